```python
import jax, jax.numpy as jnp
from jax import lax
import numpy as np

D_MODEL = 1024
BATCH = 4
SEQ = 8192
DEPTH = 2

N_MIXERS = 2
N_HEADS = 16
N_KV_HEADS = 4
HEAD_DIM = D_MODEL // N_HEADS
GROUP = N_HEADS // N_KV_HEADS
QKV_DIM = (N_HEADS + 2 * N_KV_HEADS) * HEAD_DIM
WINDOW = 128
BLOCK = 128
ROPE_THETA = 10000.0
CONV_WIDTH = 31
CONV_HALF = CONV_WIDTH // 2
D_FF = 3584
N_EXPERTS = 8
TOP_K = 2
LN_EPS = 1e-5
N_EVEN = (DEPTH + 1) // 2
N_ODD = DEPTH // 2
DN_ALPHA = (2 * DEPTH) ** 0.25
DN_BETA = (8 * DEPTH) ** -0.25
NEG_INF = -1e30

kernel_name = 'hybrid_swa_conformer_moe_deepnorm'


def layer_norm(x, g, b):
    xf = x.astype(jnp.float32)
    mu = jnp.mean(xf, axis=-1, keepdims=True)
    var = jnp.mean(jnp.square(xf - mu), axis=-1, keepdims=True)
    y = (xf - mu) * lax.rsqrt(var + LN_EPS) * g.astype(jnp.float32) + b.astype(jnp.float32)
    return y.astype(x.dtype)


def rope(t, cos, sin):
    tf = t.astype(jnp.float32)
    t1, t2 = jnp.split(tf, 2, axis=-1)
    rot = jnp.concatenate([-t2, t1], axis=-1)
    return (tf * cos[None, :, None, :] + rot * sin[None, :, None, :]).astype(t.dtype)


def windowed_gqa(x, w_qkv, sink, w_o):
    B, S, _ = x.shape
    nb = S // BLOCK
    qkv = x @ w_qkv
    q, k, v = jnp.split(qkv, [N_HEADS * HEAD_DIM, (N_HEADS + N_KV_HEADS) * HEAD_DIM], axis=-1)
    q = q.reshape(B, S, N_HEADS, HEAD_DIM)
    k = k.reshape(B, S, N_KV_HEADS, HEAD_DIM)
    v = v.reshape(B, S, N_KV_HEADS, HEAD_DIM)
    pos = jnp.arange(S, dtype=jnp.float32)
    inv_freq = ROPE_THETA ** (-jnp.arange(0, HEAD_DIM, 2, dtype=jnp.float32) / HEAD_DIM)
    ang = pos[:, None] * inv_freq[None, :]
    ang = jnp.concatenate([ang, ang], axis=-1)
    cos, sin = jnp.cos(ang), jnp.sin(ang)
    q = rope(q, cos, sin).reshape(B, nb, BLOCK, N_KV_HEADS, GROUP, HEAD_DIM)
    k = rope(k, cos, sin)

    def band(t):
        tp = jnp.pad(t, ((0, 0), (BLOCK, BLOCK), (0, 0), (0, 0)))
        tp = tp.reshape(B, nb + 2, BLOCK, N_KV_HEADS, HEAD_DIM)
        return jnp.concatenate([tp[:, :-2], tp[:, 1:-1], tp[:, 2:]], axis=2)

    kb, vb = band(k), band(v)
    s = jnp.einsum('bnqkgd,bnjkd->bnkgqj', q, kb).astype(jnp.float32) * (HEAD_DIM ** -0.5)
    blk = jnp.arange(nb)[:, None, None] * BLOCK
    qpos = blk + jnp.arange(BLOCK)[None, :, None]
    kpos = blk - BLOCK + jnp.arange(3 * BLOCK)[None, None, :]
    valid = (jnp.abs(kpos - qpos) <= WINDOW) & (kpos >= 0) & (kpos < S)
    s = jnp.where(valid[None, :, None, None], s, NEG_INF)
    sk = sink.astype(jnp.float32).reshape(N_KV_HEADS, GROUP)[None, None, :, :, None, None]
    m = jnp.maximum(jnp.max(s, axis=-1, keepdims=True), sk)
    p = jnp.exp(s - m)
    p = p / (jnp.sum(p, axis=-1, keepdims=True) + jnp.exp(sk - m))
    o = jnp.einsum('bnkgqj,bnjkd->bnqkgd', p.astype(vb.dtype), vb)
    return o.reshape(B, S, N_HEADS * HEAD_DIM) @ w_o


def conformer_conv(x, pw1_w, pw1_b, dw_w, dw_b, norm_g, norm_b, pw2_w, pw2_b):
    h = x @ pw1_w + pw1_b
    a, g = jnp.split(h, 2, axis=-1)
    h = a * jax.nn.sigmoid(g)
    h = lax.conv_general_dilated(
        h, dw_w[:, None, :].astype(h.dtype), window_strides=(1,),
        padding=[(CONV_HALF, CONV_HALF)],
        dimension_numbers=('NWC', 'WIO', 'NWC'),
        feature_group_count=D_MODEL) + dw_b
    h = jax.nn.silu(layer_norm(h, norm_g, norm_b))
    return h @ pw2_w + pw2_b


def swiglu(x, w_gate, w_up, w_down):
    return (jax.nn.silu(x @ w_gate) * (x @ w_up)) @ w_down


def moe_swiglu(x, router, w_gate, w_up, w_down):
    B, S, D = x.shape
    xt = x.reshape(B * S, D)
    logits = (xt @ router).astype(jnp.float32)
    top_v, top_i = lax.top_k(logits, TOP_K)
    gates = jax.nn.softmax(top_v, axis=-1)
    combine = jnp.sum(jax.nn.one_hot(top_i, N_EXPERTS, dtype=jnp.float32) * gates[..., None], axis=1)
    y = jnp.zeros((B * S, D), jnp.float32)
    for e in range(N_EXPERTS):
        y = y + combine[:, e:e + 1] * swiglu(xt, w_gate[e], w_up[e], w_down[e]).astype(jnp.float32)
    return y.astype(x.dtype).reshape(B, S, D)


def setup_inputs(seed: int = 0) -> dict:
    key = jax.random.key(seed)
    ks = jax.random.split(key, 24)
    f32 = jnp.float32

    def nrm(k, shape, scale):
        return jax.random.normal(k, shape, f32) * scale

    d_in = D_MODEL ** -0.5
    w_qk = nrm(ks[1], (N_EVEN, D_MODEL, (N_HEADS + N_KV_HEADS) * HEAD_DIM), d_in)
    w_v = nrm(ks[2], (N_EVEN, D_MODEL, N_KV_HEADS * HEAD_DIM), d_in * DN_BETA)
    return {
        'x': jax.random.normal(ks[0], (BATCH, SEQ, D_MODEL), f32),
        'attn_w_qkv': jnp.concatenate([w_qk, w_v], axis=-1),
        'attn_sink': nrm(ks[3], (N_EVEN, N_HEADS), 0.5),
        'attn_w_o': nrm(ks[4], (N_EVEN, N_HEADS * HEAD_DIM, D_MODEL), d_in * DN_BETA),
        'conv_pw1_w': nrm(ks[5], (N_ODD, D_MODEL, 2 * D_MODEL), d_in),
        'conv_pw1_b': nrm(ks[6], (N_ODD, 2 * D_MODEL), 0.02),
        'conv_dw_w': nrm(ks[7], (N_ODD, CONV_WIDTH, D_MODEL), CONV_WIDTH ** -0.5),
        'conv_dw_b': nrm(ks[8], (N_ODD, D_MODEL), 0.02),
        'conv_norm_g': 1.0 + nrm(ks[9], (N_ODD, D_MODEL), 0.02),
        'conv_norm_b': nrm(ks[10], (N_ODD, D_MODEL), 0.02),
        'conv_pw2_w': nrm(ks[11], (N_ODD, D_MODEL, D_MODEL), d_in * DN_BETA),
        'conv_pw2_b': nrm(ks[12], (N_ODD, D_MODEL), 0.02),
        'ffn_w_gate': nrm(ks[13], (N_EVEN, D_MODEL, D_FF), d_in),
        'ffn_w_up': nrm(ks[14], (N_EVEN, D_MODEL, D_FF), d_in),
        'ffn_w_down': nrm(ks[15], (N_EVEN, D_FF, D_MODEL), D_FF ** -0.5 * DN_BETA),
        'moe_router': nrm(ks[16], (N_ODD, D_MODEL, N_EXPERTS), d_in),
        'moe_w_gate': nrm(ks[17], (N_ODD, N_EXPERTS, D_MODEL, D_FF), d_in),
        'moe_w_up': nrm(ks[18], (N_ODD, N_EXPERTS, D_MODEL, D_FF), d_in),
        'moe_w_down': nrm(ks[19], (N_ODD, N_EXPERTS, D_FF, D_MODEL), D_FF ** -0.5 * DN_BETA),
        'ln_mix_g': 1.0 + nrm(ks[20], (DEPTH, D_MODEL), 0.02),
        'ln_mix_b': nrm(ks[21], (DEPTH, D_MODEL), 0.02),
        'ln_ffn_g': 1.0 + nrm(ks[22], (DEPTH, D_MODEL), 0.02),
        'ln_ffn_b': nrm(ks[23], (DEPTH, D_MODEL), 0.02),
    }


def reference(x, attn_w_qkv, attn_sink, attn_w_o, conv_pw1_w, conv_pw1_b, conv_dw_w, conv_dw_b,
              conv_norm_g, conv_norm_b, conv_pw2_w, conv_pw2_b, ffn_w_gate, ffn_w_up, ffn_w_down,
              moe_router, moe_w_gate, moe_w_up, moe_w_down, ln_mix_g, ln_mix_b, ln_ffn_g, ln_ffn_b):
    for i in range(DEPTH):
        j = i // N_MIXERS
        if i % N_MIXERS == 0:
            mix = windowed_gqa(x, attn_w_qkv[j], attn_sink[j], attn_w_o[j])
        else:
            mix = conformer_conv(x, conv_pw1_w[j], conv_pw1_b[j], conv_dw_w[j], conv_dw_b[j],
                                 conv_norm_g[j], conv_norm_b[j], conv_pw2_w[j], conv_pw2_b[j])
        x = layer_norm(DN_ALPHA * x + mix, ln_mix_g[i], ln_mix_b[i])
        if i % 2 == 0:
            ffn = swiglu(x, ffn_w_gate[j], ffn_w_up[j], ffn_w_down[j])
        else:
            ffn = moe_swiglu(x, moe_router[j], moe_w_gate[j], moe_w_up[j], moe_w_down[j])
        x = layer_norm(DN_ALPHA * x + ffn, ln_ffn_g[i], ln_ffn_b[i])
    return x
```

```python
import functools

import jax
import jax.numpy as jnp
from jax import lax
from jax.experimental import pallas as pl
from jax.experimental.pallas import tpu as pltpu

F32 = jnp.float32
BF16 = jnp.bfloat16

D_MODEL = 1024
N_HEADS = 16
N_KV_HEADS = 4
HEAD_DIM = D_MODEL // N_HEADS
GROUP = N_HEADS // N_KV_HEADS
Q_DIM = N_HEADS * HEAD_DIM
KV_DIM = N_KV_HEADS * HEAD_DIM
WINDOW = 128
BLOCK = 128
ROPE_THETA = 10000.0
CONV_WIDTH = 31
CONV_HALF = CONV_WIDTH // 2
D_FF = 3584
N_EXPERTS = 8
LN_EPS = 1e-5
DEPTH = 2
DN_ALPHA = (2 * DEPTH) ** 0.25
NEG_INF = -1e30

LANES = 128
SUBLANES = 8
SEGS = D_MODEL // LANES
VMEM_LIMIT = 56 * 1024 * 1024


def _params(*sem):
    return pltpu.CompilerParams(dimension_semantics=sem, vmem_limit_bytes=VMEM_LIMIT)


def _layer_norm(y, g, b):
    mu = jnp.mean(y, axis=-1, keepdims=True)
    d = y - mu
    var = jnp.mean(d * d, axis=-1, keepdims=True)
    return d * lax.rsqrt(var + LN_EPS) * g + b


def _silu(v):
    return v * jax.nn.sigmoid(v)


def _qkv_kernel(x_ref, w_ref, cos_ref, sa_ref, sb_ref, q_ref, k_ref, v_ref):
    xb = x_ref[...].astype(BF16)
    acc = jnp.dot(xb, w_ref[...], preferred_element_type=F32)
    cos = cos_ref[...]
    sa = sa_ref[...]
    sb = sb_ref[...]

    def rope(t):
        return (t * cos + pltpu.roll(t, LANES - HEAD_DIM // 2, 1) * sa
                + pltpu.roll(t, HEAD_DIM // 2, 1) * sb)

    scale = HEAD_DIM ** -0.5
    for c in range(Q_DIM // LANES):
        t = acc[:, c * LANES:(c + 1) * LANES]
        q_ref[:, c * LANES:(c + 1) * LANES] = (rope(t) * scale).astype(BF16)
    for c in range(KV_DIM // LANES):
        t = acc[:, Q_DIM + c * LANES:Q_DIM + (c + 1) * LANES]
        k_ref[:, c * LANES:(c + 1) * LANES] = rope(t).astype(BF16)
    v_ref[...] = acc[:, Q_DIM + KV_DIM:].astype(BF16)


def _qkv_rope(x2, w_qkv, seq, tm=512):
    T = x2.shape[0]
    pos = jnp.arange(seq, dtype=F32)
    inv_freq = ROPE_THETA ** (-jnp.arange(0, HEAD_DIM, 2, dtype=F32) / HEAD_DIM)
    ang = pos[:, None] * inv_freq[None, :]
    cos_h, sin_h = jnp.cos(ang), jnp.sin(ang)
    zero = jnp.zeros_like(sin_h)
    reps = LANES // HEAD_DIM
    cos = jnp.tile(jnp.concatenate([cos_h, cos_h], -1), (1, reps))
    sa = jnp.tile(jnp.concatenate([-sin_h, zero], -1), (1, reps))
    sb = jnp.tile(jnp.concatenate([zero, sin_h], -1), (1, reps))
    nseq = seq // tm
    tab = pl.BlockSpec((tm, LANES), lambda i: (i % nseq, 0))
    n_out = Q_DIM + 2 * KV_DIM
    return pl.pallas_call(
        _qkv_kernel,
        grid=(T // tm,),
        in_specs=[pl.BlockSpec((tm, D_MODEL), lambda i: (i, 0)),
                  pl.BlockSpec((D_MODEL, n_out), lambda i: (0, 0)),
                  tab, tab, tab],
        out_specs=[pl.BlockSpec((tm, Q_DIM), lambda i: (i, 0)),
                   pl.BlockSpec((tm, KV_DIM), lambda i: (i, 0)),
                   pl.BlockSpec((tm, KV_DIM), lambda i: (i, 0))],
        out_shape=[jax.ShapeDtypeStruct((T, Q_DIM), BF16),
                   jax.ShapeDtypeStruct((T, KV_DIM), BF16),
                   jax.ShapeDtypeStruct((T, KV_DIM), BF16)],
        compiler_params=_params("parallel"),
    )(x2, w_qkv, cos, sa, sb)


ATT_TQ = 512
ATT_SUB = ATT_TQ // BLOCK
ATT_KEYS = 3 * BLOCK
ATT_ROWS = GROUP * BLOCK


def _attn_kernel(sink_ref, q_ref, kp_ref, km_ref, kn_ref, vp_ref, vm_ref, vn_ref,
                 o_ref, kbuf, vbuf):
    i = pl.program_id(1)
    last = pl.num_programs(1) - 1
    kbuf[0:BLOCK] = kp_ref[...]
    kbuf[BLOCK:BLOCK + ATT_TQ] = km_ref[...]
    kbuf[BLOCK + ATT_TQ:] = kn_ref[...]
    vbuf[0:BLOCK] = vp_ref[...]
    vbuf[BLOCK:BLOCK + ATT_TQ] = vm_ref[...]
    vbuf[BLOCK + ATT_TQ:] = vn_ref[...]

    row = lax.broadcasted_iota(jnp.int32, (ATT_ROWS, ATT_KEYS), 0)
    key = lax.broadcasted_iota(jnp.int32, (ATT_ROWS, ATT_KEYS), 1)
    ql = row & (BLOCK - 1)
    band = (key >= ql) & (key <= ql + 2 * WINDOW)
    hrow = lax.broadcasted_iota(jnp.int32, (ATT_ROWS, 1), 0) // BLOCK

    def block(j, carry):
        r0 = pl.multiple_of(j * BLOCK, BLOCK)
        lo = jnp.where((i == 0) & (j == 0), BLOCK, 0)
        hi = jnp.where((i == last) & (j == ATT_SUB - 1), 2 * BLOCK, ATT_KEYS)
        valid = band & (key >= lo) & (key < hi)
        for g in range(N_KV_HEADS):
            qb = q_ref[pl.ds(r0, BLOCK), g * GROUP * HEAD_DIM:(g + 1) * GROUP * HEAD_DIM]
            q4 = jnp.concatenate(
                [qb[:, h * HEAD_DIM:(h + 1) * HEAD_DIM] for h in range(GROUP)], axis=0)
            kb = kbuf[pl.ds(r0, ATT_KEYS), g * HEAD_DIM:(g + 1) * HEAD_DIM]
            vb = vbuf[pl.ds(r0, ATT_KEYS), g * HEAD_DIM:(g + 1) * HEAD_DIM]
            s = lax.dot_general(q4, kb, (((1,), (1,)), ((), ())),
                                preferred_element_type=F32)
            s = jnp.where(valid, s, NEG_INF)
            sk = jnp.zeros((ATT_ROWS, 1), F32)
            for h in range(GROUP):
                sk = jnp.where(hrow == h, sink_ref[g * GROUP + h], sk)
            m = jnp.maximum(jnp.max(s, axis=-1, keepdims=True), sk)
            p = jnp.exp(s - m)
            den = jnp.sum(p, axis=-1, keepdims=True) + jnp.exp(sk - m)
            o = jnp.dot(p.astype(BF16), vb, preferred_element_type=F32)
            o = o / den
            o_ref[pl.ds(r0, BLOCK), g * GROUP * HEAD_DIM:(g + 1) * GROUP * HEAD_DIM] = (
                jnp.concatenate([o[h * BLOCK:(h + 1) * BLOCK] for h in range(GROUP)],
                                axis=1).astype(BF16))
        return carry

    lax.fori_loop(0, ATT_SUB, block, 0)


def _attention(q, k, v, sink, batch, seq):
    nb = seq // BLOCK
    q3 = q.reshape(batch, seq, Q_DIM)
    k3 = k.reshape(batch, seq, KV_DIM)
    v3 = v.reshape(batch, seq, KV_DIM)
    prev = pl.BlockSpec((None, BLOCK, KV_DIM),
                        lambda b, i, s: (b, jnp.maximum(i * ATT_SUB - 1, 0), 0))
    main = pl.BlockSpec((None, ATT_TQ, KV_DIM), lambda b, i, s: (b, i, 0))
    nxt = pl.BlockSpec((None, BLOCK, KV_DIM),
                       lambda b, i, s: (b, jnp.minimum((i + 1) * ATT_SUB, nb - 1), 0))
    out = pl.pallas_call(
        _attn_kernel,
        grid_spec=pltpu.PrefetchScalarGridSpec(
            num_scalar_prefetch=1,
            grid=(batch, seq // ATT_TQ),
            in_specs=[pl.BlockSpec((None, ATT_TQ, Q_DIM), lambda b, i, s: (b, i, 0)),
                      prev, main, nxt, prev, main, nxt],
            out_specs=pl.BlockSpec((None, ATT_TQ, Q_DIM), lambda b, i, s: (b, i, 0)),
            scratch_shapes=[pltpu.VMEM((ATT_TQ + 2 * BLOCK, KV_DIM), BF16),
                            pltpu.VMEM((ATT_TQ + 2 * BLOCK, KV_DIM), BF16)]),
        out_shape=jax.ShapeDtypeStruct((batch, seq, Q_DIM), BF16),
        compiler_params=_params("parallel", "parallel"),
    )(sink.astype(F32), q3, k3, k3, k3, v3, v3, v3)
    return out.reshape(batch * seq, Q_DIM)


def _proj_ln_kernel(a_ref, w_ref, bias_ref, res_ref, g_ref, b_ref, o_ref):
    y = jnp.dot(a_ref[...].astype(BF16), w_ref[...], preferred_element_type=F32)
    y = DN_ALPHA * res_ref[...] + (y + bias_ref[...])
    o_ref[...] = _layer_norm(y, g_ref[...], b_ref[...])


def _proj_ln(a, w, bias, res, g, b, tm=512):
    T, K = a.shape
    vec = pl.BlockSpec((1, D_MODEL), lambda i: (0, 0))
    return pl.pallas_call(
        _proj_ln_kernel,
        grid=(T // tm,),
        in_specs=[pl.BlockSpec((tm, K), lambda i: (i, 0)),
                  pl.BlockSpec((K, D_MODEL), lambda i: (0, 0)),
                  vec,
                  pl.BlockSpec((tm, D_MODEL), lambda i: (i, 0)),
                  vec, vec],
        out_specs=pl.BlockSpec((tm, D_MODEL), lambda i: (i, 0)),
        out_shape=jax.ShapeDtypeStruct((T, D_MODEL), F32),
        compiler_params=_params("parallel"),
    )(a, w, bias.reshape(1, -1), res, g.reshape(1, -1), b.reshape(1, -1))


FFN_FC = 512
FFN_TM = 1024


def _ffn_kernel(x_ref, wg_ref, wu_ref, wd_ref, g_ref, b_ref, o_ref, xb_scr, acc_scr):
    k = pl.program_id(1)

    @pl.when(k == 0)
    def _():
        xb_scr[...] = x_ref[...].astype(BF16)
        acc_scr[...] = jnp.zeros_like(acc_scr)

    xb = xb_scr[...]
    gate = jnp.dot(xb, wg_ref[...], preferred_element_type=F32)
    up = jnp.dot(xb, wu_ref[...], preferred_element_type=F32)
    h = (_silu(gate) * up).astype(BF16)
    acc_scr[...] += jnp.dot(h, wd_ref[...], preferred_element_type=F32)

    @pl.when(k == pl.num_programs(1) - 1)
    def _():
        y = DN_ALPHA * x_ref[...] + acc_scr[...]
        o_ref[...] = _layer_norm(y, g_ref[...], b_ref[...])


def _ffn_dense(x2, wg, wu, wd, g, b):
    T = x2.shape[0]
    tm, fc = min(FFN_TM, T), FFN_FC
    vec = pl.BlockSpec((1, D_MODEL), lambda i, k: (0, 0))
    return pl.pallas_call(
        _ffn_kernel,
        grid=(T // tm, D_FF // fc),
        in_specs=[pl.BlockSpec((tm, D_MODEL), lambda i, k: (i, 0)),
                  pl.BlockSpec((D_MODEL, fc), lambda i, k: (0, k)),
                  pl.BlockSpec((D_MODEL, fc), lambda i, k: (0, k)),
                  pl.BlockSpec((fc, D_MODEL), lambda i, k: (k, 0)),
                  vec, vec],
        out_specs=pl.BlockSpec((tm, D_MODEL), lambda i, k: (i, 0)),
        out_shape=jax.ShapeDtypeStruct((T, D_MODEL), F32),
        scratch_shapes=[pltpu.VMEM((tm, D_MODEL), BF16), pltpu.VMEM((tm, D_MODEL), F32)],
        compiler_params=_params("parallel", "arbitrary"),
    )(x2, wg, wu, wd, g.reshape(1, -1), b.reshape(1, -1))


def _pw1_glu_kernel(x_ref, w_ref, bias_ref, o_ref):
    h = jnp.dot(x_ref[...].astype(BF16), w_ref[...], preferred_element_type=F32) + bias_ref[...]
    o_ref[...] = h[:, :D_MODEL] * jax.nn.sigmoid(h[:, D_MODEL:])


def _pw1_glu(x2, w, bias, tm=512):
    T = x2.shape[0]
    return pl.pallas_call(
        _pw1_glu_kernel,
        grid=(T // tm,),
        in_specs=[pl.BlockSpec((tm, D_MODEL), lambda i: (i, 0)),
                  pl.BlockSpec((D_MODEL, 2 * D_MODEL), lambda i: (0, 0)),
                  pl.BlockSpec((1, 2 * D_MODEL), lambda i: (0, 0))],
        out_specs=pl.BlockSpec((tm, D_MODEL), lambda i: (i, 0)),
        out_shape=jax.ShapeDtypeStruct((T, D_MODEL), F32),
        compiler_params=_params("parallel"),
    )(x2, w, bias.reshape(1, -1))


CONV_TM = 256
CONV_HALO = 16
CONV_RC = 64


def _conv_kernel(hp_ref, hm_ref, hn_ref, x_ref, dw_ref, dwb_ref, ng_ref, nb_ref,
                 w2_ref, b2_ref, g_ref, b_ref, o_ref, hbuf, cbuf):
    i = pl.program_id(1)
    last = pl.num_programs(1) - 1
    tm = CONV_TM
    hbuf[0:CONV_HALO] = jnp.where(i > 0, hp_ref[...], 0.0)
    hbuf[CONV_HALO:CONV_HALO + tm] = hm_ref[...]
    hbuf[CONV_HALO + tm:] = jnp.where(i < last, hn_ref[...], 0.0)
    off = CONV_HALO - CONV_HALF
    for r in range(tm // CONV_RC):
        for c in range(SEGS):
            cols = slice(c * LANES, (c + 1) * LANES)
            acc = jnp.zeros((CONV_RC, LANES), F32)
            for w in range(CONV_WIDTH):
                r0 = r * CONV_RC + off + w
                acc = acc + hbuf[r0:r0 + CONV_RC, cols] * dw_ref[w:w + 1, cols]
            cbuf[r * CONV_RC:(r + 1) * CONV_RC, cols] = acc + dwb_ref[:, cols]
    h = _silu(_layer_norm(cbuf[...], ng_ref[...], nb_ref[...]))
    y = jnp.dot(h.astype(BF16), w2_ref[...], preferred_element_type=F32) + b2_ref[...]
    y = DN_ALPHA * x_ref[...] + y
    o_ref[...] = _layer_norm(y, g_ref[...], b_ref[...])


def _conv_block(h, x2, dw_w, dw_b, ng, nb, w2, b2, g, b, batch, seq):
    tm = CONV_TM
    h3 = h.reshape(batch, seq, D_MODEL)
    x3 = x2.reshape(batch, seq, D_MODEL)
    per = tm // CONV_HALO
    nh = seq // CONV_HALO
    vec = pl.BlockSpec((1, D_MODEL), lambda bb, i: (0, 0))
    out = pl.pallas_call(
        _conv_kernel,
        grid=(batch, seq // tm),
        in_specs=[pl.BlockSpec((None, CONV_HALO, D_MODEL),
                               lambda bb, i: (bb, jnp.maximum(i * per - 1, 0), 0)),
                  pl.BlockSpec((None, tm, D_MODEL), lambda bb, i: (bb, i, 0)),
                  pl.BlockSpec((None, CONV_HALO, D_MODEL),
                               lambda bb, i: (bb, jnp.minimum((i + 1) * per, nh - 1), 0)),
                  pl.BlockSpec((None, tm, D_MODEL), lambda bb, i: (bb, i, 0)),
                  pl.BlockSpec((CONV_WIDTH, D_MODEL), lambda bb, i: (0, 0)),
                  vec, vec, vec,
                  pl.BlockSpec((D_MODEL, D_MODEL), lambda bb, i: (0, 0)),
                  vec, vec, vec],
        out_specs=pl.BlockSpec((None, tm, D_MODEL), lambda bb, i: (bb, i, 0)),
        out_shape=jax.ShapeDtypeStruct((batch, seq, D_MODEL), F32),
        scratch_shapes=[pltpu.VMEM((tm + 2 * CONV_HALO, D_MODEL), F32),
                        pltpu.VMEM((tm, D_MODEL), F32)],
        compiler_params=_params("parallel", "parallel"),
    )(h3, h3, h3, x3, dw_w, dw_b.reshape(1, -1), ng.reshape(1, -1), nb.reshape(1, -1),
      w2, b2.reshape(1, -1), g.reshape(1, -1), b.reshape(1, -1))
    return out.reshape(batch * seq, D_MODEL)


ROUTER_TM = 512
META_E1, META_E2, META_R1, META_R2, META_G1, META_G2 = range(6)


def _router_kernel(x_ref, w_ref, meta_ref, cnt_ref, base_scr):
    t = pl.program_id(0)

    @pl.when(t == 0)
    def _():
        base_scr[...] = jnp.zeros_like(base_scr)

    tm = x_ref.shape[0]
    logits = jnp.dot(x_ref[...], w_ref[...], preferred_element_type=F32,
                     precision=lax.Precision.HIGHEST)
    lane = lax.broadcasted_iota(jnp.int32, (tm, LANES), 1)
    lg = jnp.where(lane < N_EXPERTS, logits, -jnp.inf)
    m1 = jnp.max(lg, axis=-1, keepdims=True)
    i1 = jnp.min(jnp.where(lg == m1, lane, LANES), axis=-1, keepdims=True)
    oh1 = lane == i1
    lg2 = jnp.where(oh1, -jnp.inf, lg)
    m2 = jnp.max(lg2, axis=-1, keepdims=True)
    i2 = jnp.min(jnp.where(lg2 == m2, lane, LANES), axis=-1, keepdims=True)
    oh2 = lane == i2
    e = jnp.exp(m2 - m1)
    g1 = 1.0 / (1.0 + e)
    g2 = e / (1.0 + e)
    oh = jnp.where(oh1 | oh2, 1.0, 0.0)
    r = lax.broadcasted_iota(jnp.int32, (tm, tm), 0)
    c = lax.broadcasted_iota(jnp.int32, (tm, tm), 1)
    lower = jnp.where(c < r, 1.0, 0.0).astype(BF16)
    before = jnp.dot(lower, oh.astype(BF16), preferred_element_type=F32) + base_scr[...]
    rank1 = jnp.sum(jnp.where(oh1, before, 0.0), axis=-1, keepdims=True)
    rank2 = jnp.sum(jnp.where(oh2, before, 0.0), axis=-1, keepdims=True)
    base_scr[...] += jnp.sum(oh, axis=0, keepdims=True)
    meta = jnp.zeros((tm, LANES), F32)
    for idx, val in ((META_E1, i1.astype(F32)), (META_E2, i2.astype(F32)),
                     (META_R1, rank1), (META_R2, rank2), (META_G1, g1), (META_G2, g2)):
        meta = jnp.where(lane == idx, val, meta)
    meta_ref[...] = meta
    cnt_ref[...] = base_scr[...]


def _router(x2, w_router):
    T = x2.shape[0]
    tm = min(ROUTER_TM, T)
    w = jnp.zeros((D_MODEL, LANES), F32).at[:, :N_EXPERTS].set(w_router.astype(F32))
    return pl.pallas_call(
        _router_kernel,
        grid=(T // tm,),
        in_specs=[pl.BlockSpec((tm, D_MODEL), lambda i: (i, 0)),
                  pl.BlockSpec((D_MODEL, LANES), lambda i: (0, 0))],
        out_specs=[pl.BlockSpec((tm, LANES), lambda i: (i, 0)),
                   pl.BlockSpec((1, LANES), lambda i: (0, 0))],
        out_shape=[jax.ShapeDtypeStruct((T, LANES), F32),
                   jax.ShapeDtypeStruct((1, LANES), F32)],
        scratch_shapes=[pltpu.VMEM((1, LANES), F32)],
        compiler_params=_params("arbitrary"),
    )(x2, w)


DISP_TS = 256
DISP_ZB = 128


def _row_copy(src, src_row8, dst, dst_row8, sem):
    return pltpu.make_async_copy(
        src.at[pl.ds(pl.multiple_of(src_row8, SUBLANES), SEGS), :],
        dst.at[pl.ds(pl.multiple_of(dst_row8, SUBLANES), SEGS), :], sem)


def _dispatch_kernel(pad0_ref, padn_ref, tail_ref, p1_ref, p2_ref, x_ref, xs_ref,
                     sbuf, zbuf, sem, zsem):
    ts = x_ref.shape[0]
    for s in range(SEGS):
        sbuf[pl.ds(s, ts, stride=SEGS), :] = x_ref[:, s * LANES:(s + 1) * LANES]

    def start(r, carry):
        _row_copy(sbuf, r * SEGS, xs_ref, p1_ref[r], sem).start()
        _row_copy(sbuf, r * SEGS, xs_ref, p2_ref[r], sem).start()
        return carry

    def wait(r, carry):
        _row_copy(sbuf, r * SEGS, xs_ref, p1_ref[r], sem).wait()
        _row_copy(sbuf, r * SEGS, xs_ref, p2_ref[r], sem).wait()
        return carry

    lax.fori_loop(0, ts, start, 0)

    @pl.when(pl.program_id(0) == 0)
    def _():
        zbuf[...] = jnp.zeros_like(zbuf)
        for e in range(N_EXPERTS):
            def zstart(j, carry, e=e):
                _row_copy(zbuf, 0, xs_ref, pad0_ref[e] + j * SEGS, zsem).start()
                return carry

            def zwait(j, carry, e=e):
                _row_copy(zbuf, 0, xs_ref, pad0_ref[e] + j * SEGS, zsem).wait()
                return carry

            lax.fori_loop(0, padn_ref[e], zstart, 0)
            lax.fori_loop(0, padn_ref[e], zwait, 0)

        def tail_copy(j):
            rows = DISP_ZB * SEGS
            dst0 = pl.multiple_of(tail_ref[0] + j * rows, SUBLANES)
            return pltpu.make_async_copy(zbuf, xs_ref.at[pl.ds(dst0, rows), :], zsem)

        def tstart(j, carry):
            tail_copy(j).start()
            return carry

        def twait(j, carry):
            tail_copy(j).wait()
            return carry

        lax.fori_loop(0, tail_ref[1], tstart, 0)
        lax.fori_loop(0, tail_ref[1], twait, 0)

    lax.fori_loop(0, ts, wait, 0)


def _dispatch(x2, pos1_8, pos2_8, pad0_8, padn, tail, n_slots):
    T = x2.shape[0]
    ts = min(DISP_TS, T)
    smem = lambda: pl.BlockSpec((ts,), lambda i, *_: (i,), memory_space=pltpu.SMEM)
    return pl.pallas_call(
        _dispatch_kernel,
        grid_spec=pltpu.PrefetchScalarGridSpec(
            num_scalar_prefetch=3,
            grid=(T // ts,),
            in_specs=[smem(), smem(),
                      pl.BlockSpec((ts, D_MODEL), lambda i, *_: (i, 0))],
            out_specs=pl.BlockSpec(memory_space=pl.ANY),
            scratch_shapes=[pltpu.VMEM((ts * SEGS, LANES), F32),
                            pltpu.VMEM((DISP_ZB * SEGS, LANES), F32),
                            pltpu.SemaphoreType.DMA(()),
                            pltpu.SemaphoreType.DMA(())]),
        out_shape=jax.ShapeDtypeStruct((n_slots * SEGS, LANES), F32),
        compiler_params=_params("arbitrary"),
    )(pad0_8, padn, tail, pos1_8, pos2_8, x2)


MOE_TM = 1024
MOE_FC = 512


def _moe_kernel(te_ref, nu_ref, xs_ref, wg_ref, wu_ref, wd_ref, o_ref, xb_scr, acc_scr):
    i = pl.program_id(0)
    k = pl.program_id(1)
    tm = xb_scr.shape[0]

    @pl.when(i < nu_ref[0])
    def _():
        @pl.when(k == 0)
        def _():
            for s in range(SEGS):
                xb_scr[:, s * LANES:(s + 1) * LANES] = (
                    xs_ref[pl.ds(s, tm, stride=SEGS), :].astype(BF16))
            acc_scr[...] = jnp.zeros_like(acc_scr)

        xb = xb_scr[...]
        gate = jnp.dot(xb, wg_ref[...], preferred_element_type=F32)
        up = jnp.dot(xb, wu_ref[...], preferred_element_type=F32)
        h = (_silu(gate) * up).astype(BF16)
        acc_scr[...] += jnp.dot(h, wd_ref[...], preferred_element_type=F32)

        @pl.when(k == pl.num_programs(1) - 1)
        def _():
            for s in range(SEGS):
                o_ref[pl.ds(s, tm, stride=SEGS), :] = acc_scr[:, s * LANES:(s + 1) * LANES]

    @pl.when((i >= nu_ref[0]) & (k == 0))
    def _():
        o_ref[...] = jnp.zeros_like(o_ref)


def _moe(xs, tile_expert, n_used, wg, wu, wd, tm):
    n_tiles = xs.shape[0] // (tm * SEGS)
    fc = MOE_FC
    nk = D_FF // fc

    def tile(i, k, te, nu):
        return (jnp.minimum(i, nu[0] - 1), 0)

    def kk(i, k, nu):
        return jnp.where(i < nu[0], k, nk - 1)

    return pl.pallas_call(
        _moe_kernel,
        grid_spec=pltpu.PrefetchScalarGridSpec(
            num_scalar_prefetch=2,
            grid=(n_tiles, nk),
            in_specs=[pl.BlockSpec((tm * SEGS, LANES), tile),
                      pl.BlockSpec((None, D_MODEL, fc),
                                   lambda i, k, te, nu: (te[i], 0, kk(i, k, nu))),
                      pl.BlockSpec((None, D_MODEL, fc),
                                   lambda i, k, te, nu: (te[i], 0, kk(i, k, nu))),
                      pl.BlockSpec((None, fc, D_MODEL),
                                   lambda i, k, te, nu: (te[i], kk(i, k, nu), 0))],
            out_specs=pl.BlockSpec((tm * SEGS, LANES), lambda i, k, te, nu: (i, 0)),
            scratch_shapes=[pltpu.VMEM((tm, D_MODEL), BF16), pltpu.VMEM((tm, D_MODEL), F32)]),
        out_shape=jax.ShapeDtypeStruct(xs.shape, F32),
        compiler_params=_params("arbitrary", "arbitrary"),
    )(tile_expert, n_used, xs, wg, wu, wd)


COMB_TS = 256


def _combine_kernel(p1_ref, p2_ref, ys_ref, meta_ref, x_ref, g_ref, b_ref, o_ref,
                    buf1, buf2, sem):
    ts = x_ref.shape[0]

    def start(r, carry):
        _row_copy(ys_ref, p1_ref[r], buf1, r * SEGS, sem).start()
        _row_copy(ys_ref, p2_ref[r], buf2, r * SEGS, sem).start()
        return carry

    def wait(r, carry):
        _row_copy(ys_ref, p1_ref[r], buf1, r * SEGS, sem).wait()
        _row_copy(ys_ref, p2_ref[r], buf2, r * SEGS, sem).wait()
        return carry

    lax.fori_loop(0, ts, start, 0)
    lax.fori_loop(0, ts, wait, 0)
    meta = meta_ref[...]
    g1 = meta[:, META_G1:META_G1 + 1]
    g2 = meta[:, META_G2:META_G2 + 1]
    for s in range(SEGS):
        cols = slice(s * LANES, (s + 1) * LANES)
        y = (g1 * buf1[pl.ds(s, ts, stride=SEGS), :]
             + g2 * buf2[pl.ds(s, ts, stride=SEGS), :])
        o_ref[:, cols] = DN_ALPHA * x_ref[:, cols] + y
    o_ref[...] = _layer_norm(o_ref[...], g_ref[...], b_ref[...])


def _combine(ys, pos1_8, pos2_8, meta, x2, g, b):
    T = x2.shape[0]
    ts = min(COMB_TS, T)
    smem = lambda: pl.BlockSpec((ts,), lambda i: (i,), memory_space=pltpu.SMEM)
    vec = pl.BlockSpec((1, D_MODEL), lambda i: (0, 0))
    return pl.pallas_call(
        _combine_kernel,
        grid=(T // ts,),
        in_specs=[smem(), smem(),
                  pl.BlockSpec(memory_space=pl.ANY),
                  pl.BlockSpec((ts, LANES), lambda i: (i, 0)),
                  pl.BlockSpec((ts, D_MODEL), lambda i: (i, 0)),
                  vec, vec],
        out_specs=pl.BlockSpec((ts, D_MODEL), lambda i: (i, 0)),
        out_shape=jax.ShapeDtypeStruct((T, D_MODEL), F32),
        scratch_shapes=[pltpu.VMEM((ts * SEGS, LANES), F32),
                        pltpu.VMEM((ts * SEGS, LANES), F32),
                        pltpu.SemaphoreType.DMA(())],
        compiler_params=_params("parallel"),
    )(pos1_8, pos2_8, ys, meta, x2, g.reshape(1, -1), b.reshape(1, -1))


def _moe_block(x2, w_router, wg, wu, wd, g, b):
    T = x2.shape[0]
    tm = min(MOE_TM, T)
    meta, cnt = _router(x2, w_router)
    col = lambda idx: meta[:, idx].astype(jnp.int32)
    e1, e2, r1, r2 = col(META_E1), col(META_E2), col(META_R1), col(META_R2)
    counts = cnt[0, :N_EXPERTS].astype(jnp.int32)
    tiles_e = (counts + tm - 1) // tm
    padded = tiles_e * tm
    starts = jnp.cumsum(padded) - padded
    tile_end = jnp.cumsum(tiles_e)
    n_used = tile_end[-1]
    n_tiles = (TOPK_SLOTS * T) // tm + N_EXPERTS
    tid = jnp.minimum(jnp.arange(n_tiles, dtype=jnp.int32), n_used - 1)
    tile_expert = jnp.sum((tid[:, None] >= tile_end[None, :]).astype(jnp.int32), axis=1)
    pos1_8 = (starts[e1] + r1) * SEGS
    pos2_8 = (starts[e2] + r2) * SEGS
    pad0_8 = (starts + counts) * SEGS
    padn = padded - counts
    tail = jnp.stack([n_used * (tm * SEGS), (n_tiles - n_used) * (tm // DISP_ZB)])
    xs = _dispatch(x2, pos1_8, pos2_8, pad0_8, padn, tail, n_tiles * tm)
    ys = _moe(xs, tile_expert, n_used.reshape(1), wg, wu, wd, tm)
    return _combine(ys, pos1_8, pos2_8, meta, x2, g, b)


TOPK_SLOTS = 2


def kernel(x, attn_w_qkv, attn_sink, attn_w_o, conv_pw1_w, conv_pw1_b, conv_dw_w, conv_dw_b,
           conv_norm_g, conv_norm_b, conv_pw2_w, conv_pw2_b, ffn_w_gate, ffn_w_up, ffn_w_down,
           moe_router, moe_w_gate, moe_w_up, moe_w_down, ln_mix_g, ln_mix_b, ln_ffn_g, ln_ffn_b):
    batch, seq, _ = x.shape
    x2 = x.reshape(batch * seq, D_MODEL)
    zero_bias = jnp.zeros((D_MODEL,), F32)

    q, k, v = _qkv_rope(x2, attn_w_qkv[0].astype(BF16), seq)
    att = _attention(q, k, v, attn_sink[0], batch, seq)
    x2 = _proj_ln(att, attn_w_o[0].astype(BF16), zero_bias, x2, ln_mix_g[0], ln_mix_b[0])
    x2 = _ffn_dense(x2, ffn_w_gate[0].astype(BF16), ffn_w_up[0].astype(BF16),
                    ffn_w_down[0].astype(BF16), ln_ffn_g[0], ln_ffn_b[0])

    h = _pw1_glu(x2, conv_pw1_w[0].astype(BF16), conv_pw1_b[0])
    x2 = _conv_block(h, x2, conv_dw_w[0], conv_dw_b[0], conv_norm_g[0], conv_norm_b[0],
                     conv_pw2_w[0].astype(BF16), conv_pw2_b[0], ln_mix_g[1], ln_mix_b[1],
                     batch, seq)
    x2 = _moe_block(x2, moe_router[0], moe_w_gate[0].astype(BF16), moe_w_up[0].astype(BF16),
                    moe_w_down[0].astype(BF16), ln_ffn_g[1], ln_ffn_b[1])
    return x2.reshape(batch, seq, D_MODEL)
```

```python
import functools

import jax
import jax.numpy as jnp
from jax import lax
from jax.experimental import pallas as pl
from jax.experimental.pallas import tpu as pltpu

F32 = jnp.float32
BF16 = jnp.bfloat16

D_MODEL = 1024
N_HEADS = 16
N_KV_HEADS = 4
HEAD_DIM = D_MODEL // N_HEADS
GROUP = N_HEADS // N_KV_HEADS
Q_DIM = N_HEADS * HEAD_DIM
KV_DIM = N_KV_HEADS * HEAD_DIM
WINDOW = 128
BLOCK = 128
ROPE_THETA = 10000.0
CONV_WIDTH = 31
CONV_HALF = CONV_WIDTH // 2
D_FF = 3584
N_EXPERTS = 8
LN_EPS = 1e-5
DEPTH = 2
DN_ALPHA = (2 * DEPTH) ** 0.25
NEG_INF = -1e30

LOG2E = 1.4426950408889634
LANES = 128
SUBLANES = 8
SEGS = D_MODEL // LANES
VAUG_DIM = N_KV_HEADS * LANES
VMEM_LIMIT = 56 * 1024 * 1024


def _params(*sem):
    return pltpu.CompilerParams(dimension_semantics=sem, vmem_limit_bytes=VMEM_LIMIT)


def _layer_norm(y, g, b):
    mu = jnp.mean(y, axis=-1, keepdims=True)
    d = y - mu
    var = jnp.mean(d * d, axis=-1, keepdims=True)
    return d * lax.rsqrt(var + LN_EPS) * g + b


def _silu(v):
    return v * jax.nn.sigmoid(v)


def _qkv_kernel(x_ref, w_ref, cos_ref, sa_ref, sb_ref, q_ref, k_ref, v_ref):
    xb = x_ref[...].astype(BF16)
    acc = jnp.dot(xb, w_ref[...], preferred_element_type=F32)
    cos = cos_ref[...]
    sa = sa_ref[...]
    sb = sb_ref[...]

    def rope(t):
        return (t * cos + pltpu.roll(t, LANES - HEAD_DIM // 2, 1) * sa
                + pltpu.roll(t, HEAD_DIM // 2, 1) * sb)

    scale = HEAD_DIM ** -0.5 * LOG2E
    for c in range(Q_DIM // LANES):
        t = acc[:, c * LANES:(c + 1) * LANES]
        q_ref[:, c * LANES:(c + 1) * LANES] = (rope(t) * scale).astype(BF16)
    for c in range(KV_DIM // LANES):
        t = acc[:, Q_DIM + c * LANES:Q_DIM + (c + 1) * LANES]
        k_ref[:, c * LANES:(c + 1) * LANES] = rope(t).astype(BF16)
    ones = jnp.ones((acc.shape[0], HEAD_DIM), F32)
    for g in range(N_KV_HEADS):
        vg = acc[:, Q_DIM + KV_DIM + g * HEAD_DIM:Q_DIM + KV_DIM + (g + 1) * HEAD_DIM]
        v_ref[:, g * LANES:(g + 1) * LANES] = jnp.concatenate([vg, ones], axis=1).astype(BF16)


def _qkv_rope(x2, w_qkv, seq, tm=512):
    T = x2.shape[0]
    pos = jnp.arange(seq, dtype=F32)
    inv_freq = ROPE_THETA ** (-jnp.arange(0, HEAD_DIM, 2, dtype=F32) / HEAD_DIM)
    ang = pos[:, None] * inv_freq[None, :]
    cos_h, sin_h = jnp.cos(ang), jnp.sin(ang)
    zero = jnp.zeros_like(sin_h)
    reps = LANES // HEAD_DIM
    cos = jnp.tile(jnp.concatenate([cos_h, cos_h], -1), (1, reps))
    sa = jnp.tile(jnp.concatenate([-sin_h, zero], -1), (1, reps))
    sb = jnp.tile(jnp.concatenate([zero, sin_h], -1), (1, reps))
    nseq = seq // tm
    tab = pl.BlockSpec((tm, LANES), lambda i: (i % nseq, 0))
    n_out = Q_DIM + 2 * KV_DIM
    return pl.pallas_call(
        _qkv_kernel,
        grid=(T // tm,),
        in_specs=[pl.BlockSpec((tm, D_MODEL), lambda i: (i, 0)),
                  pl.BlockSpec((D_MODEL, n_out), lambda i: (0, 0)),
                  tab, tab, tab],
        out_specs=[pl.BlockSpec((tm, Q_DIM), lambda i: (i, 0)),
                   pl.BlockSpec((tm, KV_DIM), lambda i: (i, 0)),
                   pl.BlockSpec((tm, VAUG_DIM), lambda i: (i, 0))],
        out_shape=[jax.ShapeDtypeStruct((T, Q_DIM), BF16),
                   jax.ShapeDtypeStruct((T, KV_DIM), BF16),
                   jax.ShapeDtypeStruct((T, VAUG_DIM), BF16)],
        compiler_params=_params("parallel"),
    )(x2, w_qkv, cos, sa, sb)


ATT_TQ = 512
ATT_SUB = ATT_TQ // BLOCK
ATT_KEYS = 3 * BLOCK
ATT_ROWS = GROUP * BLOCK


def _attn_kernel(sink_ref, q_ref, kp_ref, km_ref, kn_ref, vp_ref, vm_ref, vn_ref,
                 o_ref, kbuf, vbuf):
    i = pl.program_id(1)
    last = pl.num_programs(1) - 1
    kbuf[0:BLOCK] = kp_ref[...]
    kbuf[BLOCK:BLOCK + ATT_TQ] = km_ref[...]
    kbuf[BLOCK + ATT_TQ:] = kn_ref[...]
    vbuf[0:BLOCK] = vp_ref[...]
    vbuf[BLOCK:BLOCK + ATT_TQ] = vm_ref[...]
    vbuf[BLOCK + ATT_TQ:] = vn_ref[...]

    row = lax.broadcasted_iota(jnp.int32, (ATT_ROWS, BLOCK), 0)
    key = lax.broadcasted_iota(jnp.int32, (ATT_ROWS, BLOCK), 1)
    ql = row & (BLOCK - 1)
    band_prev = key >= ql
    band_next = key <= ql
    hrow = lax.broadcasted_iota(jnp.int32, (ATT_ROWS, 1), 0) // BLOCK

    def block(j, carry):
        r0 = pl.multiple_of(j * BLOCK, BLOCK)
        has_prev = jnp.logical_not((i == 0) & (j == 0))
        has_next = jnp.logical_not((i == last) & (j == ATT_SUB - 1))
        valid_prev = band_prev & has_prev
        valid_next = band_next & has_next
        for g in range(N_KV_HEADS):
            qb = q_ref[pl.ds(r0, BLOCK), g * GROUP * HEAD_DIM:(g + 1) * GROUP * HEAD_DIM]
            q4 = jnp.concatenate(
                [qb[:, h * HEAD_DIM:(h + 1) * HEAD_DIM] for h in range(GROUP)], axis=0)
            kb = kbuf[pl.ds(r0, ATT_KEYS), g * HEAD_DIM:(g + 1) * HEAD_DIM]
            vb = vbuf[pl.ds(r0, ATT_KEYS), g * LANES:(g + 1) * LANES]
            s = lax.dot_general(q4, kb, (((1,), (1,)), ((), ())),
                                preferred_element_type=F32)
            s = jnp.concatenate(
                [jnp.where(valid_prev, s[:, :BLOCK], NEG_INF), s[:, BLOCK:2 * BLOCK],
                 jnp.where(valid_next, s[:, 2 * BLOCK:], NEG_INF)], axis=1)
            sk = jnp.zeros((ATT_ROWS, 1), F32)
            for h in range(GROUP):
                sk = jnp.where(hrow == h, sink_ref[g * GROUP + h] * LOG2E, sk)
            m = jnp.maximum(jnp.max(s, axis=-1, keepdims=True), sk)
            p = jnp.exp2(s - m)
            oa = jnp.dot(p.astype(BF16), vb, preferred_element_type=F32)
            den = oa[:, HEAD_DIM:] + jnp.exp2(sk - m)
            o = oa[:, :HEAD_DIM] / den
            o_ref[pl.ds(r0, BLOCK), g * GROUP * HEAD_DIM:(g + 1) * GROUP * HEAD_DIM] = (
                jnp.concatenate([o[h * BLOCK:(h + 1) * BLOCK] for h in range(GROUP)],
                                axis=1).astype(BF16))
        return carry

    lax.fori_loop(0, ATT_SUB, block, 0)


def _attention(q, k, v, sink, batch, seq):
    nb = seq // BLOCK
    q3 = q.reshape(batch, seq, Q_DIM)
    k3 = k.reshape(batch, seq, KV_DIM)
    v3 = v.reshape(batch, seq, VAUG_DIM)
    prev = lambda w: pl.BlockSpec((None, BLOCK, w),
                                  lambda b, i, s: (b, jnp.maximum(i * ATT_SUB - 1, 0), 0))
    main = lambda w: pl.BlockSpec((None, ATT_TQ, w), lambda b, i, s: (b, i, 0))
    nxt = lambda w: pl.BlockSpec(
        (None, BLOCK, w), lambda b, i, s: (b, jnp.minimum((i + 1) * ATT_SUB, nb - 1), 0))
    out = pl.pallas_call(
        _attn_kernel,
        grid_spec=pltpu.PrefetchScalarGridSpec(
            num_scalar_prefetch=1,
            grid=(batch, seq // ATT_TQ),
            in_specs=[pl.BlockSpec((None, ATT_TQ, Q_DIM), lambda b, i, s: (b, i, 0)),
                      prev(KV_DIM), main(KV_DIM), nxt(KV_DIM),
                      prev(VAUG_DIM), main(VAUG_DIM), nxt(VAUG_DIM)],
            out_specs=pl.BlockSpec((None, ATT_TQ, Q_DIM), lambda b, i, s: (b, i, 0)),
            scratch_shapes=[pltpu.VMEM((ATT_TQ + 2 * BLOCK, KV_DIM), BF16),
                            pltpu.VMEM((ATT_TQ + 2 * BLOCK, VAUG_DIM), BF16)]),
        out_shape=jax.ShapeDtypeStruct((batch, seq, Q_DIM), BF16),
        compiler_params=_params("parallel", "parallel"),
    )(sink.astype(F32), q3, k3, k3, k3, v3, v3, v3)
    return out.reshape(batch * seq, Q_DIM)


def _proj_ln_kernel(a_ref, w_ref, bias_ref, res_ref, g_ref, b_ref, o_ref):
    y = jnp.dot(a_ref[...].astype(BF16), w_ref[...], preferred_element_type=F32)
    y = DN_ALPHA * res_ref[...] + (y + bias_ref[...])
    o_ref[...] = _layer_norm(y, g_ref[...], b_ref[...])


def _proj_ln(a, w, bias, res, g, b, tm=512):
    T, K = a.shape
    vec = pl.BlockSpec((1, D_MODEL), lambda i: (0, 0))
    return pl.pallas_call(
        _proj_ln_kernel,
        grid=(T // tm,),
        in_specs=[pl.BlockSpec((tm, K), lambda i: (i, 0)),
                  pl.BlockSpec((K, D_MODEL), lambda i: (0, 0)),
                  vec,
                  pl.BlockSpec((tm, D_MODEL), lambda i: (i, 0)),
                  vec, vec],
        out_specs=pl.BlockSpec((tm, D_MODEL), lambda i: (i, 0)),
        out_shape=jax.ShapeDtypeStruct((T, D_MODEL), F32),
        compiler_params=_params("parallel"),
    )(a, w, bias.reshape(1, -1), res, g.reshape(1, -1), b.reshape(1, -1))


FFN_FC = 512
FFN_TM = 1024


def _ffn_kernel(x_ref, wg_ref, wu_ref, wd_ref, g_ref, b_ref, o_ref, xb_scr, acc_scr):
    k = pl.program_id(1)

    @pl.when(k == 0)
    def _():
        xb_scr[...] = x_ref[...].astype(BF16)
        acc_scr[...] = jnp.zeros_like(acc_scr)

    xb = xb_scr[...]
    gate = jnp.dot(xb, wg_ref[...], preferred_element_type=F32)
    up = jnp.dot(xb, wu_ref[...], preferred_element_type=F32)
    h = (_silu(gate) * up).astype(BF16)
    acc_scr[...] += jnp.dot(h, wd_ref[...], preferred_element_type=F32)

    @pl.when(k == pl.num_programs(1) - 1)
    def _():
        y = DN_ALPHA * x_ref[...] + acc_scr[...]
        o_ref[...] = _layer_norm(y, g_ref[...], b_ref[...])


def _ffn_dense(x2, wg, wu, wd, g, b):
    T = x2.shape[0]
    tm, fc = min(FFN_TM, T), FFN_FC
    vec = pl.BlockSpec((1, D_MODEL), lambda i, k: (0, 0))
    return pl.pallas_call(
        _ffn_kernel,
        grid=(T // tm, D_FF // fc),
        in_specs=[pl.BlockSpec((tm, D_MODEL), lambda i, k: (i, 0)),
                  pl.BlockSpec((D_MODEL, fc), lambda i, k: (0, k)),
                  pl.BlockSpec((D_MODEL, fc), lambda i, k: (0, k)),
                  pl.BlockSpec((fc, D_MODEL), lambda i, k: (k, 0)),
                  vec, vec],
        out_specs=pl.BlockSpec((tm, D_MODEL), lambda i, k: (i, 0)),
        out_shape=jax.ShapeDtypeStruct((T, D_MODEL), F32),
        scratch_shapes=[pltpu.VMEM((tm, D_MODEL), BF16), pltpu.VMEM((tm, D_MODEL), F32)],
        compiler_params=_params("parallel", "arbitrary"),
    )(x2, wg, wu, wd, g.reshape(1, -1), b.reshape(1, -1))


def _pw1_glu_kernel(x_ref, w_ref, bias_ref, o_ref):
    h = jnp.dot(x_ref[...].astype(BF16), w_ref[...], preferred_element_type=F32) + bias_ref[...]
    o_ref[...] = h[:, :D_MODEL] * jax.nn.sigmoid(h[:, D_MODEL:])


def _pw1_glu(x2, w, bias, tm=512):
    T = x2.shape[0]
    return pl.pallas_call(
        _pw1_glu_kernel,
        grid=(T // tm,),
        in_specs=[pl.BlockSpec((tm, D_MODEL), lambda i: (i, 0)),
                  pl.BlockSpec((D_MODEL, 2 * D_MODEL), lambda i: (0, 0)),
                  pl.BlockSpec((1, 2 * D_MODEL), lambda i: (0, 0))],
        out_specs=pl.BlockSpec((tm, D_MODEL), lambda i: (i, 0)),
        out_shape=jax.ShapeDtypeStruct((T, D_MODEL), F32),
        compiler_params=_params("parallel"),
    )(x2, w, bias.reshape(1, -1))


CONV_TM = 512
CONV_HALO = 16
CONV_RC = 64
CONV_PARTS = 2
CONV_SHIFT_ROWS = CONV_TM + 2 * CONV_HALO - SUBLANES


def _conv_kernel(hp_ref, hm_ref, hn_ref, x_ref, dw_ref, dwb_ref, ng_ref, nb_ref,
                 w2_ref, b2_ref, g_ref, b_ref, o_ref, hbuf, hshift, cbuf):
    i = pl.program_id(1)
    last = pl.num_programs(1) - 1
    tm = CONV_TM
    hbuf[0:CONV_HALO] = jnp.where(i > 0, hp_ref[...], 0.0)
    hbuf[CONV_HALO:CONV_HALO + tm] = hm_ref[...]
    hbuf[CONV_HALO + tm:] = jnp.where(i < last, hn_ref[...], 0.0)
    for sh in range(1, SUBLANES):
        hshift[sh - 1] = hbuf[sh:sh + CONV_SHIFT_ROWS, :]
    off = CONV_HALO - CONV_HALF

    def rows(r, carry):
        r0 = pl.multiple_of(r * CONV_RC, CONV_RC)
        for c in range(SEGS):
            cols = slice(c * LANES, (c + 1) * LANES)
            parts = [None] * CONV_PARTS
            for sh in range(SUBLANES):
                taps = [w for w in range(CONV_WIDTH) if (off + w) % SUBLANES == sh]
                src = hbuf if sh == 0 else hshift.at[sh - 1]
                span = (off + taps[-1]) // SUBLANES * SUBLANES + CONV_RC
                seg = src[pl.ds(r0, span), cols]
                for w in taps:
                    lo = (off + w) // SUBLANES * SUBLANES
                    term = seg[lo:lo + CONV_RC] * dw_ref[w:w + 1, cols]
                    k = w % CONV_PARTS
                    parts[k] = term if parts[k] is None else parts[k] + term
            acc = parts[0]
            for part in parts[1:]:
                acc = acc + part
            cbuf[pl.ds(r0, CONV_RC), cols] = acc + dwb_ref[:, cols]
        return carry

    lax.fori_loop(0, tm // CONV_RC, rows, 0)
    h = _silu(_layer_norm(cbuf[...], ng_ref[...], nb_ref[...]))
    y = jnp.dot(h.astype(BF16), w2_ref[...], preferred_element_type=F32) + b2_ref[...]
    y = DN_ALPHA * x_ref[...] + y
    o_ref[...] = _layer_norm(y, g_ref[...], b_ref[...])


def _conv_block(h, x2, dw_w, dw_b, ng, nb, w2, b2, g, b, batch, seq):
    tm = CONV_TM
    h3 = h.reshape(batch, seq, D_MODEL)
    x3 = x2.reshape(batch, seq, D_MODEL)
    per = tm // CONV_HALO
    nh = seq // CONV_HALO
    vec = pl.BlockSpec((1, D_MODEL), lambda bb, i: (0, 0))
    out = pl.pallas_call(
        _conv_kernel,
        grid=(batch, seq // tm),
        in_specs=[pl.BlockSpec((None, CONV_HALO, D_MODEL),
                               lambda bb, i: (bb, jnp.maximum(i * per - 1, 0), 0)),
                  pl.BlockSpec((None, tm, D_MODEL), lambda bb, i: (bb, i, 0)),
                  pl.BlockSpec((None, CONV_HALO, D_MODEL),
                               lambda bb, i: (bb, jnp.minimum((i + 1) * per, nh - 1), 0)),
                  pl.BlockSpec((None, tm, D_MODEL), lambda bb, i: (bb, i, 0)),
                  pl.BlockSpec((CONV_WIDTH, D_MODEL), lambda bb, i: (0, 0)),
                  vec, vec, vec,
                  pl.BlockSpec((D_MODEL, D_MODEL), lambda bb, i: (0, 0)),
                  vec, vec, vec],
        out_specs=pl.BlockSpec((None, tm, D_MODEL), lambda bb, i: (bb, i, 0)),
        out_shape=jax.ShapeDtypeStruct((batch, seq, D_MODEL), F32),
        scratch_shapes=[pltpu.VMEM((tm + 2 * CONV_HALO, D_MODEL), F32),
                        pltpu.VMEM((SUBLANES - 1, CONV_SHIFT_ROWS, D_MODEL), F32),
                        pltpu.VMEM((tm, D_MODEL), F32)],
        compiler_params=_params("parallel", "parallel"),
    )(h3, h3, h3, x3, dw_w, dw_b.reshape(1, -1), ng.reshape(1, -1), nb.reshape(1, -1),
      w2, b2.reshape(1, -1), g.reshape(1, -1), b.reshape(1, -1))
    return out.reshape(batch * seq, D_MODEL)


ROUTER_TM = 512
META_E1, META_E2, META_R1, META_R2, META_G1, META_G2 = range(6)


def _router_kernel(x_ref, w_ref, meta_ref, cnt_ref, base_scr):
    t = pl.program_id(0)

    @pl.when(t == 0)
    def _():
        base_scr[...] = jnp.zeros_like(base_scr)

    tm = x_ref.shape[0]
    x = x_ref[...]
    w = w_ref[...]
    xh = x.astype(BF16)
    xl = (x - xh.astype(F32)).astype(BF16)
    wh = w.astype(BF16)
    wl = (w - wh.astype(F32)).astype(BF16)
    logits = (jnp.dot(xh, wh, preferred_element_type=F32)
              + (jnp.dot(xh, wl, preferred_element_type=F32)
                 + jnp.dot(xl, wh, preferred_element_type=F32)))
    lane = lax.broadcasted_iota(jnp.int32, (tm, LANES), 1)
    lg = jnp.where(lane < N_EXPERTS, logits, -jnp.inf)
    m1 = jnp.max(lg, axis=-1, keepdims=True)
    i1 = jnp.min(jnp.where(lg == m1, lane, LANES), axis=-1, keepdims=True)
    oh1 = lane == i1
    lg2 = jnp.where(oh1, -jnp.inf, lg)
    m2 = jnp.max(lg2, axis=-1, keepdims=True)
    i2 = jnp.min(jnp.where(lg2 == m2, lane, LANES), axis=-1, keepdims=True)
    oh2 = lane == i2
    e = jnp.exp(m2 - m1)
    g1 = 1.0 / (1.0 + e)
    g2 = e / (1.0 + e)
    oh = jnp.where(oh1 | oh2, 1.0, 0.0)
    r = lax.broadcasted_iota(jnp.int32, (tm, tm), 0)
    c = lax.broadcasted_iota(jnp.int32, (tm, tm), 1)
    lower = jnp.where(c < r, 1.0, 0.0).astype(BF16)
    before = jnp.dot(lower, oh.astype(BF16), preferred_element_type=F32) + base_scr[...]
    rank1 = jnp.sum(jnp.where(oh1, before, 0.0), axis=-1, keepdims=True)
    rank2 = jnp.sum(jnp.where(oh2, before, 0.0), axis=-1, keepdims=True)
    base_scr[...] += jnp.sum(oh, axis=0, keepdims=True)
    meta = jnp.zeros((tm, LANES), F32)
    for idx, val in ((META_E1, i1.astype(F32)), (META_E2, i2.astype(F32)),
                     (META_R1, rank1), (META_R2, rank2), (META_G1, g1), (META_G2, g2)):
        meta = jnp.where(lane == idx, val, meta)
    meta_ref[...] = meta
    cnt_ref[...] = base_scr[...]


def _router(x2, w_router):
    T = x2.shape[0]
    tm = min(ROUTER_TM, T)
    w = jnp.zeros((D_MODEL, LANES), F32).at[:, :N_EXPERTS].set(w_router.astype(F32))
    return pl.pallas_call(
        _router_kernel,
        grid=(T // tm,),
        in_specs=[pl.BlockSpec((tm, D_MODEL), lambda i: (i, 0)),
                  pl.BlockSpec((D_MODEL, LANES), lambda i: (0, 0))],
        out_specs=[pl.BlockSpec((tm, LANES), lambda i: (i, 0)),
                   pl.BlockSpec((1, LANES), lambda i: (0, 0))],
        out_shape=[jax.ShapeDtypeStruct((T, LANES), F32),
                   jax.ShapeDtypeStruct((1, LANES), F32)],
        scratch_shapes=[pltpu.VMEM((1, LANES), F32)],
        compiler_params=_params("arbitrary"),
    )(x2, w)


DISP_TS = 256
DISP_ZB = 128


def _row_copy(src, src_row8, dst, dst_row8, sem):
    return pltpu.make_async_copy(
        src.at[pl.ds(pl.multiple_of(src_row8, SUBLANES), SEGS), :],
        dst.at[pl.ds(pl.multiple_of(dst_row8, SUBLANES), SEGS), :], sem)


def _dispatch_kernel(pad0_ref, padn_ref, tail_ref, p1_ref, p2_ref, x_ref, xs_ref,
                     sbuf, zbuf, sem, zsem):
    i = pl.program_id(0)
    ts = x_ref.shape[0]
    slot = i % 2
    stage = sbuf.at[slot]
    for s in range(SEGS):
        stage[pl.ds(s, ts, stride=SEGS), :] = x_ref[:, s * LANES:(s + 1) * LANES]

    def start(r, carry):
        _row_copy(stage, r * SEGS, xs_ref, p1_ref[r], sem.at[slot]).start(priority=0)
        _row_copy(stage, r * SEGS, xs_ref, p2_ref[r], sem.at[slot]).start(priority=1)
        return carry

    def wait_step(which):
        for _ in range(2):
            pltpu.make_async_copy(sbuf.at[which], xs_ref.at[pl.ds(0, ts * SEGS), :],
                                  sem.at[which]).wait()

    lax.fori_loop(0, ts, start, 0)

    @pl.when(pl.program_id(0) == 0)
    def _():
        zbuf[...] = jnp.zeros_like(zbuf)
        for e in range(N_EXPERTS):
            def zstart(j, carry, e=e):
                _row_copy(zbuf, 0, xs_ref, pad0_ref[e] + j * SEGS, zsem).start()
                return carry

            def zwait(j, carry, e=e):
                _row_copy(zbuf, 0, xs_ref, pad0_ref[e] + j * SEGS, zsem).wait()
                return carry

            lax.fori_loop(0, padn_ref[e], zstart, 0)
            lax.fori_loop(0, padn_ref[e], zwait, 0)

        def tail_copy(j):
            rows = DISP_ZB * SEGS
            dst0 = pl.multiple_of(tail_ref[0] + j * rows, SUBLANES)
            return pltpu.make_async_copy(zbuf, xs_ref.at[pl.ds(dst0, rows), :], zsem)

        def tstart(j, carry):
            tail_copy(j).start()
            return carry

        def twait(j, carry):
            tail_copy(j).wait()
            return carry

        lax.fori_loop(0, tail_ref[1], tstart, 0)
        lax.fori_loop(0, tail_ref[1], twait, 0)

    @pl.when(i > 0)
    def _():
        wait_step(1 - slot)

    @pl.when(i == pl.num_programs(0) - 1)
    def _():
        wait_step(slot)


def _dispatch(x2, pos1_8, pos2_8, pad0_8, padn, tail, n_slots):
    T = x2.shape[0]
    ts = min(DISP_TS, T)
    smem = lambda: pl.BlockSpec((ts,), lambda i, *_: (i,), memory_space=pltpu.SMEM)
    return pl.pallas_call(
        _dispatch_kernel,
        grid_spec=pltpu.PrefetchScalarGridSpec(
            num_scalar_prefetch=3,
            grid=(T // ts,),
            in_specs=[smem(), smem(),
                      pl.BlockSpec((ts, D_MODEL), lambda i, *_: (i, 0))],
            out_specs=pl.BlockSpec(memory_space=pl.ANY),
            scratch_shapes=[pltpu.VMEM((2, ts * SEGS, LANES), F32),
                            pltpu.VMEM((DISP_ZB * SEGS, LANES), F32),
                            pltpu.SemaphoreType.DMA((2,)),
                            pltpu.SemaphoreType.DMA(())]),
        out_shape=jax.ShapeDtypeStruct((n_slots * SEGS, LANES), F32),
        compiler_params=_params("arbitrary"),
    )(pad0_8, padn, tail, pos1_8, pos2_8, x2)


MOE_TM = 1024
MOE_FC = 512


def _moe_kernel(te_ref, nu_ref, xs_ref, wg_ref, wu_ref, wd_ref, o_ref, xb_scr, acc_scr):
    i = pl.program_id(0)
    k = pl.program_id(1)
    tm = xb_scr.shape[0]

    @pl.when(i < nu_ref[0])
    def _():
        @pl.when(k == 0)
        def _():
            for s in range(SEGS):
                xb_scr[:, s * LANES:(s + 1) * LANES] = (
                    xs_ref[pl.ds(s, tm, stride=SEGS), :].astype(BF16))
            acc_scr[...] = jnp.zeros_like(acc_scr)

        xb = xb_scr[...]
        gate = jnp.dot(xb, wg_ref[...], preferred_element_type=F32)
        up = jnp.dot(xb, wu_ref[...], preferred_element_type=F32)
        h = (_silu(gate) * up).astype(BF16)
        acc_scr[...] += jnp.dot(h, wd_ref[...], preferred_element_type=F32)

        @pl.when(k == pl.num_programs(1) - 1)
        def _():
            for s in range(SEGS):
                o_ref[pl.ds(s, tm, stride=SEGS), :] = acc_scr[:, s * LANES:(s + 1) * LANES]

    @pl.when((i >= nu_ref[0]) & (k == 0))
    def _():
        o_ref[...] = jnp.zeros_like(o_ref)


def _moe(xs, tile_expert, n_used, wg, wu, wd, tm):
    n_tiles = xs.shape[0] // (tm * SEGS)
    fc = MOE_FC
    nk = D_FF // fc

    def tile(i, k, te, nu):
        return (jnp.minimum(i, nu[0] - 1), 0)

    def kk(i, k, nu):
        return jnp.where(i < nu[0], k, nk - 1)

    return pl.pallas_call(
        _moe_kernel,
        grid_spec=pltpu.PrefetchScalarGridSpec(
            num_scalar_prefetch=2,
            grid=(n_tiles, nk),
            in_specs=[pl.BlockSpec((tm * SEGS, LANES), tile),
                      pl.BlockSpec((None, D_MODEL, fc),
                                   lambda i, k, te, nu: (te[i], 0, kk(i, k, nu))),
                      pl.BlockSpec((None, D_MODEL, fc),
                                   lambda i, k, te, nu: (te[i], 0, kk(i, k, nu))),
                      pl.BlockSpec((None, fc, D_MODEL),
                                   lambda i, k, te, nu: (te[i], kk(i, k, nu), 0))],
            out_specs=pl.BlockSpec((tm * SEGS, LANES), lambda i, k, te, nu: (i, 0)),
            scratch_shapes=[pltpu.VMEM((tm, D_MODEL), BF16), pltpu.VMEM((tm, D_MODEL), F32)]),
        out_shape=jax.ShapeDtypeStruct(xs.shape, F32),
        compiler_params=_params("arbitrary", "arbitrary"),
    )(tile_expert, n_used, xs, wg, wu, wd)


COMB_TS = 256


def _combine_kernel(p1_ref, p2_ref, p1n_ref, p2n_ref, ys_ref, meta_ref, x_ref, g_ref, b_ref,
                    o_ref, buf1, buf2, sem):
    i = pl.program_id(0)
    ts = x_ref.shape[0]
    slot = i % 2

    def gather(pa_ref, pb_ref, which):
        def start(r, carry):
            _row_copy(ys_ref, pa_ref[r], buf1.at[which], r * SEGS,
                      sem.at[which]).start(priority=0)
            _row_copy(ys_ref, pb_ref[r], buf2.at[which], r * SEGS,
                      sem.at[which]).start(priority=1)
            return carry
        lax.fori_loop(0, ts, start, 0)

    @pl.when(i == 0)
    def _():
        gather(p1_ref, p2_ref, slot)

    @pl.when(i < pl.num_programs(0) - 1)
    def _():
        gather(p1n_ref, p2n_ref, 1 - slot)

    for buf in (buf1, buf2):
        pltpu.make_async_copy(ys_ref.at[pl.ds(0, ts * SEGS), :], buf.at[slot],
                              sem.at[slot]).wait()
    meta = meta_ref[...]
    g1 = meta[:, META_G1:META_G1 + 1]
    g2 = meta[:, META_G2:META_G2 + 1]
    rows1 = buf1.at[slot]
    rows2 = buf2.at[slot]
    for s in range(SEGS):
        cols = slice(s * LANES, (s + 1) * LANES)
        y = (g1 * rows1[pl.ds(s, ts, stride=SEGS), :]
             + g2 * rows2[pl.ds(s, ts, stride=SEGS), :])
        o_ref[:, cols] = DN_ALPHA * x_ref[:, cols] + y
    o_ref[...] = _layer_norm(o_ref[...], g_ref[...], b_ref[...])


def _combine(ys, pos1_8, pos2_8, meta, x2, g, b):
    T = x2.shape[0]
    ts = min(COMB_TS, T)
    n = T // ts
    smem = lambda: pl.BlockSpec((ts,), lambda i: (i,), memory_space=pltpu.SMEM)
    smem_next = lambda: pl.BlockSpec((ts,), lambda i: (jnp.minimum(i + 1, n - 1),),
                                     memory_space=pltpu.SMEM)
    vec = pl.BlockSpec((1, D_MODEL), lambda i: (0, 0))
    return pl.pallas_call(
        _combine_kernel,
        grid=(n,),
        in_specs=[smem(), smem(), smem_next(), smem_next(),
                  pl.BlockSpec(memory_space=pl.ANY),
                  pl.BlockSpec((ts, LANES), lambda i: (i, 0)),
                  pl.BlockSpec((ts, D_MODEL), lambda i: (i, 0)),
                  vec, vec],
        out_specs=pl.BlockSpec((ts, D_MODEL), lambda i: (i, 0)),
        out_shape=jax.ShapeDtypeStruct((T, D_MODEL), F32),
        scratch_shapes=[pltpu.VMEM((2, ts * SEGS, LANES), F32),
                        pltpu.VMEM((2, ts * SEGS, LANES), F32),
                        pltpu.SemaphoreType.DMA((2,))],
        compiler_params=_params("arbitrary"),
    )(pos1_8, pos2_8, pos1_8, pos2_8, ys, meta, x2, g.reshape(1, -1), b.reshape(1, -1))


def _moe_block(x2, w_router, wg, wu, wd, g, b):
    T = x2.shape[0]
    tm = min(MOE_TM, T)
    meta, cnt = _router(x2, w_router)
    col = lambda idx: meta[:, idx].astype(jnp.int32)
    e1, e2, r1, r2 = col(META_E1), col(META_E2), col(META_R1), col(META_R2)
    counts = cnt[0, :N_EXPERTS].astype(jnp.int32)
    tiles_e = (counts + tm - 1) // tm
    padded = tiles_e * tm
    starts = jnp.cumsum(padded) - padded
    tile_end = jnp.cumsum(tiles_e)
    n_used = tile_end[-1]
    n_tiles = (TOPK_SLOTS * T) // tm + N_EXPERTS
    tid = jnp.minimum(jnp.arange(n_tiles, dtype=jnp.int32), n_used - 1)
    tile_expert = jnp.sum((tid[:, None] >= tile_end[None, :]).astype(jnp.int32), axis=1)
    pos1_8 = (starts[e1] + r1) * SEGS
    pos2_8 = (starts[e2] + r2) * SEGS
    pad0_8 = (starts + counts) * SEGS
    padn = padded - counts
    tail = jnp.stack([n_used * (tm * SEGS), (n_tiles - n_used) * (tm // DISP_ZB)])
    xs = _dispatch(x2, pos1_8, pos2_8, pad0_8, padn, tail, n_tiles * tm)
    ys = _moe(xs, tile_expert, n_used.reshape(1), wg, wu, wd, tm)
    return _combine(ys, pos1_8, pos2_8, meta, x2, g, b)


TOPK_SLOTS = 2


def kernel(x, attn_w_qkv, attn_sink, attn_w_o, conv_pw1_w, conv_pw1_b, conv_dw_w, conv_dw_b,
           conv_norm_g, conv_norm_b, conv_pw2_w, conv_pw2_b, ffn_w_gate, ffn_w_up, ffn_w_down,
           moe_router, moe_w_gate, moe_w_up, moe_w_down, ln_mix_g, ln_mix_b, ln_ffn_g, ln_ffn_b):
    batch, seq, _ = x.shape
    x2 = x.reshape(batch * seq, D_MODEL)
    zero_bias = jnp.zeros((D_MODEL,), F32)

    q, k, v = _qkv_rope(x2, attn_w_qkv[0].astype(BF16), seq)
    att = _attention(q, k, v, attn_sink[0], batch, seq)
    x2 = _proj_ln(att, attn_w_o[0].astype(BF16), zero_bias, x2, ln_mix_g[0], ln_mix_b[0])
    x2 = _ffn_dense(x2, ffn_w_gate[0].astype(BF16), ffn_w_up[0].astype(BF16),
                    ffn_w_down[0].astype(BF16), ln_ffn_g[0], ln_ffn_b[0])

    h = _pw1_glu(x2, conv_pw1_w[0].astype(BF16), conv_pw1_b[0])
    x2 = _conv_block(h, x2, conv_dw_w[0], conv_dw_b[0], conv_norm_g[0], conv_norm_b[0],
                     conv_pw2_w[0].astype(BF16), conv_pw2_b[0], ln_mix_g[1], ln_mix_b[1],
                     batch, seq)
    x2 = _moe_block(x2, moe_router[0], moe_w_gate[0].astype(BF16), moe_w_up[0].astype(BF16),
                    moe_w_down[0].astype(BF16), ln_ffn_g[1], ln_ffn_b[1])
    return x2.reshape(batch, seq, D_MODEL)
```

```python
import jax
import jax.numpy as jnp
from jax import lax
from jax.experimental import pallas as pl
from jax.experimental.pallas import tpu as pltpu

F32 = jnp.float32
BF16 = jnp.bfloat16

D_MODEL = 1024
N_HEADS = 16
N_KV_HEADS = 4
HEAD_DIM = D_MODEL // N_HEADS
GROUP = N_HEADS // N_KV_HEADS
Q_DIM = N_HEADS * HEAD_DIM
KV_DIM = N_KV_HEADS * HEAD_DIM
WINDOW = 128
BLOCK = 128
ROPE_THETA = 10000.0
CONV_WIDTH = 31
CONV_HALF = CONV_WIDTH // 2
D_FF = 3584
N_EXPERTS = 8
LN_EPS = 1e-5
DEPTH = 2
DN_ALPHA = (2 * DEPTH) ** 0.25
NEG_INF = -1e30

LOG2E = 1.4426950408889634
LANES = 128
SUBLANES = 8
SEGS = D_MODEL // LANES
VAUG_DIM = N_KV_HEADS * LANES
VMEM_LIMIT = 56 * 1024 * 1024


def _params(*sem):
    return pltpu.CompilerParams(dimension_semantics=sem, vmem_limit_bytes=VMEM_LIMIT)


def _layer_norm(y, g, b):
    mu = jnp.mean(y, axis=-1, keepdims=True)
    d = y - mu
    var = jnp.mean(d * d, axis=-1, keepdims=True)
    return d * lax.rsqrt(var + LN_EPS) * g + b


def _silu(v):
    return v * jax.nn.sigmoid(v)


def _qkv_kernel(x_ref, w_ref, cos_ref, sa_ref, sb_ref, q_ref, k_ref, v_ref):
    xb = x_ref[...].astype(BF16)
    acc = jnp.dot(xb, w_ref[...], preferred_element_type=F32)
    cos = cos_ref[...]
    sa = sa_ref[...]
    sb = sb_ref[...]

    def rope(t):
        return (t * cos + pltpu.roll(t, LANES - HEAD_DIM // 2, 1) * sa
                + pltpu.roll(t, HEAD_DIM // 2, 1) * sb)

    scale = HEAD_DIM ** -0.5 * LOG2E
    for c in range(Q_DIM // LANES):
        t = acc[:, c * LANES:(c + 1) * LANES]
        q_ref[:, c * LANES:(c + 1) * LANES] = (rope(t) * scale).astype(BF16)
    for c in range(KV_DIM // LANES):
        t = acc[:, Q_DIM + c * LANES:Q_DIM + (c + 1) * LANES]
        k_ref[:, c * LANES:(c + 1) * LANES] = rope(t).astype(BF16)
    ones = jnp.ones((acc.shape[0], HEAD_DIM), F32)
    for g in range(N_KV_HEADS):
        vg = acc[:, Q_DIM + KV_DIM + g * HEAD_DIM:Q_DIM + KV_DIM + (g + 1) * HEAD_DIM]
        v_ref[:, g * LANES:(g + 1) * LANES] = jnp.concatenate([vg, ones], axis=1).astype(BF16)


def _qkv_rope(x2, w_qkv, seq, tm=512):
    T = x2.shape[0]
    pos = jnp.arange(seq, dtype=F32)
    inv_freq = ROPE_THETA ** (-jnp.arange(0, HEAD_DIM, 2, dtype=F32) / HEAD_DIM)
    ang = pos[:, None] * inv_freq[None, :]
    cos_h, sin_h = jnp.cos(ang), jnp.sin(ang)
    zero = jnp.zeros_like(sin_h)
    reps = LANES // HEAD_DIM
    cos = jnp.tile(jnp.concatenate([cos_h, cos_h], -1), (1, reps))
    sa = jnp.tile(jnp.concatenate([-sin_h, zero], -1), (1, reps))
    sb = jnp.tile(jnp.concatenate([zero, sin_h], -1), (1, reps))
    nseq = seq // tm
    tab = pl.BlockSpec((tm, LANES), lambda i: (i % nseq, 0))
    n_out = Q_DIM + 2 * KV_DIM
    return pl.pallas_call(
        _qkv_kernel,
        grid=(T // tm,),
        in_specs=[pl.BlockSpec((tm, D_MODEL), lambda i: (i, 0)),
                  pl.BlockSpec((D_MODEL, n_out), lambda i: (0, 0)),
                  tab, tab, tab],
        out_specs=[pl.BlockSpec((tm, Q_DIM), lambda i: (i, 0)),
                   pl.BlockSpec((tm, KV_DIM), lambda i: (i, 0)),
                   pl.BlockSpec((tm, VAUG_DIM), lambda i: (i, 0))],
        out_shape=[jax.ShapeDtypeStruct((T, Q_DIM), BF16),
                   jax.ShapeDtypeStruct((T, KV_DIM), BF16),
                   jax.ShapeDtypeStruct((T, VAUG_DIM), BF16)],
        compiler_params=_params("parallel"),
    )(x2, w_qkv, cos, sa, sb)


ATT_TQ = 512
ATT_SUB = ATT_TQ // BLOCK
ATT_KEYS = 3 * BLOCK
ATT_ROWS = GROUP * BLOCK


def _attn_kernel(sink_ref, q_ref, kp_ref, km_ref, kn_ref, vp_ref, vm_ref, vn_ref,
                 o_ref, kbuf, vbuf):
    i = pl.program_id(1)
    last = pl.num_programs(1) - 1
    kbuf[0:BLOCK] = kp_ref[...]
    kbuf[BLOCK:BLOCK + ATT_TQ] = km_ref[...]
    kbuf[BLOCK + ATT_TQ:] = kn_ref[...]
    vbuf[0:BLOCK] = vp_ref[...]
    vbuf[BLOCK:BLOCK + ATT_TQ] = vm_ref[...]
    vbuf[BLOCK + ATT_TQ:] = vn_ref[...]

    row = lax.broadcasted_iota(jnp.int32, (ATT_ROWS, BLOCK), 0)
    key = lax.broadcasted_iota(jnp.int32, (ATT_ROWS, BLOCK), 1)
    ql = row & (BLOCK - 1)
    band_prev = key >= ql
    band_next = key <= ql
    hrow = lax.broadcasted_iota(jnp.int32, (ATT_ROWS, 1), 0) // BLOCK

    def block(j, carry):
        r0 = pl.multiple_of(j * BLOCK, BLOCK)
        has_prev = jnp.logical_not((i == 0) & (j == 0))
        has_next = jnp.logical_not((i == last) & (j == ATT_SUB - 1))
        valid_prev = band_prev & has_prev
        valid_next = band_next & has_next
        for g in range(N_KV_HEADS):
            qb = q_ref[pl.ds(r0, BLOCK), g * GROUP * HEAD_DIM:(g + 1) * GROUP * HEAD_DIM]
            q4 = jnp.concatenate(
                [qb[:, h * HEAD_DIM:(h + 1) * HEAD_DIM] for h in range(GROUP)], axis=0)
            kb = kbuf[pl.ds(r0, ATT_KEYS), g * HEAD_DIM:(g + 1) * HEAD_DIM]
            vb = vbuf[pl.ds(r0, ATT_KEYS), g * LANES:(g + 1) * LANES]
            s = lax.dot_general(q4, kb, (((1,), (1,)), ((), ())),
                                preferred_element_type=F32)
            s = jnp.concatenate(
                [jnp.where(valid_prev, s[:, :BLOCK], NEG_INF), s[:, BLOCK:2 * BLOCK],
                 jnp.where(valid_next, s[:, 2 * BLOCK:], NEG_INF)], axis=1)
            sk = jnp.zeros((ATT_ROWS, 1), F32)
            for h in range(GROUP):
                sk = jnp.where(hrow == h, sink_ref[g * GROUP + h] * LOG2E, sk)
            m = jnp.maximum(jnp.max(s, axis=-1, keepdims=True), sk)
            p = jnp.exp2(s - m)
            oa = jnp.dot(p.astype(BF16), vb, preferred_element_type=F32)
            den = oa[:, HEAD_DIM:] + jnp.exp2(sk - m)
            o = oa[:, :HEAD_DIM] / den
            o_ref[pl.ds(r0, BLOCK), g * GROUP * HEAD_DIM:(g + 1) * GROUP * HEAD_DIM] = (
                jnp.concatenate([o[h * BLOCK:(h + 1) * BLOCK] for h in range(GROUP)],
                                axis=1).astype(BF16))
        return carry

    lax.fori_loop(0, ATT_SUB, block, 0)


def _attention(q, k, v, sink, batch, seq):
    nb = seq // BLOCK
    q3 = q.reshape(batch, seq, Q_DIM)
    k3 = k.reshape(batch, seq, KV_DIM)
    v3 = v.reshape(batch, seq, VAUG_DIM)
    prev = lambda w: pl.BlockSpec((None, BLOCK, w),
                                  lambda b, i, s: (b, jnp.maximum(i * ATT_SUB - 1, 0), 0))
    main = lambda w: pl.BlockSpec((None, ATT_TQ, w), lambda b, i, s: (b, i, 0))
    nxt = lambda w: pl.BlockSpec(
        (None, BLOCK, w), lambda b, i, s: (b, jnp.minimum((i + 1) * ATT_SUB, nb - 1), 0))
    out = pl.pallas_call(
        _attn_kernel,
        grid_spec=pltpu.PrefetchScalarGridSpec(
            num_scalar_prefetch=1,
            grid=(batch, seq // ATT_TQ),
            in_specs=[pl.BlockSpec((None, ATT_TQ, Q_DIM), lambda b, i, s: (b, i, 0)),
                      prev(KV_DIM), main(KV_DIM), nxt(KV_DIM),
                      prev(VAUG_DIM), main(VAUG_DIM), nxt(VAUG_DIM)],
            out_specs=pl.BlockSpec((None, ATT_TQ, Q_DIM), lambda b, i, s: (b, i, 0)),
            scratch_shapes=[pltpu.VMEM((ATT_TQ + 2 * BLOCK, KV_DIM), BF16),
                            pltpu.VMEM((ATT_TQ + 2 * BLOCK, VAUG_DIM), BF16)]),
        out_shape=jax.ShapeDtypeStruct((batch, seq, Q_DIM), BF16),
        compiler_params=_params("parallel", "parallel"),
    )(sink.astype(F32), q3, k3, k3, k3, v3, v3, v3)
    return out.reshape(batch * seq, Q_DIM)


FFN_FC = 512
FFN_TM = 1024


def _swiglu_step(xb, wg_ref, wu_ref, wd_ref):
    gate = jnp.dot(xb, wg_ref[...].astype(BF16), preferred_element_type=F32)
    up = jnp.dot(xb, wu_ref[...].astype(BF16), preferred_element_type=F32)
    h = (_silu(gate) * up).astype(BF16)
    return jnp.dot(h, wd_ref[...].astype(BF16), preferred_element_type=F32)


def _attn_out_ffn_kernel(att_ref, x_ref, wo_ref, g1_ref, b1_ref, wg_ref, wu_ref, wd_ref,
                         g2_ref, b2_ref, o_ref, x1_scr, xb_scr, acc_scr):
    k = pl.program_id(1)

    @pl.when(k == 0)
    def _():
        y = jnp.dot(att_ref[...], wo_ref[...], preferred_element_type=F32)
        x1 = _layer_norm(DN_ALPHA * x_ref[...] + y, g1_ref[...], b1_ref[...])
        x1_scr[...] = x1
        xb_scr[...] = x1.astype(BF16)
        acc_scr[...] = jnp.zeros_like(acc_scr)

    acc_scr[...] += _swiglu_step(xb_scr[...], wg_ref, wu_ref, wd_ref)

    @pl.when(k == pl.num_programs(1) - 1)
    def _():
        y = DN_ALPHA * x1_scr[...] + acc_scr[...]
        o_ref[...] = _layer_norm(y, g2_ref[...], b2_ref[...])


def _attn_out_ffn(att, x2, wo, g1, b1, wg, wu, wd, g2, b2):
    T = x2.shape[0]
    tm, fc = min(FFN_TM, T), FFN_FC
    vec = pl.BlockSpec((1, D_MODEL), lambda i, k: (0, 0))
    row = pl.BlockSpec((tm, D_MODEL), lambda i, k: (i, 0))
    return pl.pallas_call(
        _attn_out_ffn_kernel,
        grid=(T // tm, D_FF // fc),
        in_specs=[row, row,
                  pl.BlockSpec((D_MODEL, D_MODEL), lambda i, k: (0, 0)),
                  vec, vec,
                  pl.BlockSpec((D_MODEL, fc), lambda i, k: (0, k)),
                  pl.BlockSpec((D_MODEL, fc), lambda i, k: (0, k)),
                  pl.BlockSpec((fc, D_MODEL), lambda i, k: (k, 0)),
                  vec, vec],
        out_specs=row,
        out_shape=jax.ShapeDtypeStruct((T, D_MODEL), F32),
        scratch_shapes=[pltpu.VMEM((tm, D_MODEL), F32), pltpu.VMEM((tm, D_MODEL), BF16),
                        pltpu.VMEM((tm, D_MODEL), F32)],
        compiler_params=_params("parallel", "arbitrary"),
    )(att, x2, wo, g1.reshape(1, -1), b1.reshape(1, -1), wg, wu, wd,
      g2.reshape(1, -1), b2.reshape(1, -1))


def _pw1_glu_kernel(x_ref, w_ref, bias_ref, o_ref):
    h = jnp.dot(x_ref[...].astype(BF16), w_ref[...], preferred_element_type=F32) + bias_ref[...]
    o_ref[...] = h[:, :D_MODEL] * jax.nn.sigmoid(h[:, D_MODEL:])


def _pw1_glu(x2, w, bias, tm=512):
    T = x2.shape[0]
    return pl.pallas_call(
        _pw1_glu_kernel,
        grid=(T // tm,),
        in_specs=[pl.BlockSpec((tm, D_MODEL), lambda i: (i, 0)),
                  pl.BlockSpec((D_MODEL, 2 * D_MODEL), lambda i: (0, 0)),
                  pl.BlockSpec((1, 2 * D_MODEL), lambda i: (0, 0))],
        out_specs=pl.BlockSpec((tm, D_MODEL), lambda i: (i, 0)),
        out_shape=jax.ShapeDtypeStruct((T, D_MODEL), F32),
        compiler_params=_params("parallel"),
    )(x2, w, bias.reshape(1, -1))


CONV_TM = 512
CONV_HALO = 16
CONV_RC = 64
CONV_PARTS = 2
CONV_SHIFT_ROWS = CONV_TM + 2 * CONV_HALO - SUBLANES


def _conv_kernel(hp_ref, hm_ref, hn_ref, x_ref, dw_ref, dwb_ref, ng_ref, nb_ref,
                 w2_ref, b2_ref, g_ref, b_ref, o_ref, hbuf, hshift, cbuf):
    i = pl.program_id(1)
    last = pl.num_programs(1) - 1
    tm = CONV_TM
    hbuf[0:CONV_HALO] = jnp.where(i > 0, hp_ref[...], 0.0)
    hbuf[CONV_HALO:CONV_HALO + tm] = hm_ref[...]
    hbuf[CONV_HALO + tm:] = jnp.where(i < last, hn_ref[...], 0.0)
    for sh in range(1, SUBLANES):
        hshift[sh - 1] = hbuf[sh:sh + CONV_SHIFT_ROWS, :]
    off = CONV_HALO - CONV_HALF

    def rows(r, carry):
        r0 = pl.multiple_of(r * CONV_RC, CONV_RC)
        for c in range(SEGS):
            cols = slice(c * LANES, (c + 1) * LANES)
            parts = [None] * CONV_PARTS
            for sh in range(SUBLANES):
                taps = [w for w in range(CONV_WIDTH) if (off + w) % SUBLANES == sh]
                src = hbuf if sh == 0 else hshift.at[sh - 1]
                span = (off + taps[-1]) // SUBLANES * SUBLANES + CONV_RC
                seg = src[pl.ds(r0, span), cols]
                for w in taps:
                    lo = (off + w) // SUBLANES * SUBLANES
                    term = seg[lo:lo + CONV_RC] * dw_ref[w:w + 1, cols]
                    k = w % CONV_PARTS
                    parts[k] = term if parts[k] is None else parts[k] + term
            acc = parts[0]
            for part in parts[1:]:
                acc = acc + part
            cbuf[pl.ds(r0, CONV_RC), cols] = acc + dwb_ref[:, cols]
        return carry

    lax.fori_loop(0, tm // CONV_RC, rows, 0)
    h = _silu(_layer_norm(cbuf[...], ng_ref[...], nb_ref[...]))
    y = jnp.dot(h.astype(BF16), w2_ref[...], preferred_element_type=F32) + b2_ref[...]
    y = DN_ALPHA * x_ref[...] + y
    o_ref[...] = _layer_norm(y, g_ref[...], b_ref[...])


def _conv_block(h, x2, dw_w, dw_b, ng, nb, w2, b2, g, b, batch, seq):
    tm = CONV_TM
    h3 = h.reshape(batch, seq, D_MODEL)
    x3 = x2.reshape(batch, seq, D_MODEL)
    per = tm // CONV_HALO
    nh = seq // CONV_HALO
    vec = pl.BlockSpec((1, D_MODEL), lambda bb, i: (0, 0))
    out = pl.pallas_call(
        _conv_kernel,
        grid=(batch, seq // tm),
        in_specs=[pl.BlockSpec((None, CONV_HALO, D_MODEL),
                               lambda bb, i: (bb, jnp.maximum(i * per - 1, 0), 0)),
                  pl.BlockSpec((None, tm, D_MODEL), lambda bb, i: (bb, i, 0)),
                  pl.BlockSpec((None, CONV_HALO, D_MODEL),
                               lambda bb, i: (bb, jnp.minimum((i + 1) * per, nh - 1), 0)),
                  pl.BlockSpec((None, tm, D_MODEL), lambda bb, i: (bb, i, 0)),
                  pl.BlockSpec((CONV_WIDTH, D_MODEL), lambda bb, i: (0, 0)),
                  vec, vec, vec,
                  pl.BlockSpec((D_MODEL, D_MODEL), lambda bb, i: (0, 0)),
                  vec, vec, vec],
        out_specs=pl.BlockSpec((None, tm, D_MODEL), lambda bb, i: (bb, i, 0)),
        out_shape=jax.ShapeDtypeStruct((batch, seq, D_MODEL), F32),
        scratch_shapes=[pltpu.VMEM((tm + 2 * CONV_HALO, D_MODEL), F32),
                        pltpu.VMEM((SUBLANES - 1, CONV_SHIFT_ROWS, D_MODEL), F32),
                        pltpu.VMEM((tm, D_MODEL), F32)],
        compiler_params=_params("parallel", "parallel"),
    )(h3, h3, h3, x3, dw_w, dw_b.reshape(1, -1), ng.reshape(1, -1), nb.reshape(1, -1),
      w2, b2.reshape(1, -1), g.reshape(1, -1), b.reshape(1, -1))
    return out.reshape(batch * seq, D_MODEL)


ROUTER_TM = 512
META_E1, META_E2, META_R1, META_R2, META_G1, META_G2 = range(6)


def _router_kernel(x_ref, w_ref, meta_ref, meta_t_ref, cnt_ref, base_scr):
    t = pl.program_id(0)

    @pl.when(t == 0)
    def _():
        base_scr[...] = jnp.zeros_like(base_scr)

    tm = x_ref.shape[0]
    x = x_ref[...]
    w = w_ref[...]
    xh = x.astype(BF16)
    xl = (x - xh.astype(F32)).astype(BF16)
    wh = w.astype(BF16)
    wl = (w - wh.astype(F32)).astype(BF16)
    logits = (jnp.dot(xh, wh, preferred_element_type=F32)
              + (jnp.dot(xh, wl, preferred_element_type=F32)
                 + jnp.dot(xl, wh, preferred_element_type=F32)))
    lane = lax.broadcasted_iota(jnp.int32, (tm, LANES), 1)
    lg = jnp.where(lane < N_EXPERTS, logits, -jnp.inf)
    m1 = jnp.max(lg, axis=-1, keepdims=True)
    i1 = jnp.min(jnp.where(lg == m1, lane, LANES), axis=-1, keepdims=True)
    oh1 = lane == i1
    lg2 = jnp.where(oh1, -jnp.inf, lg)
    m2 = jnp.max(lg2, axis=-1, keepdims=True)
    i2 = jnp.min(jnp.where(lg2 == m2, lane, LANES), axis=-1, keepdims=True)
    oh2 = lane == i2
    e = jnp.exp(m2 - m1)
    g1 = 1.0 / (1.0 + e)
    g2 = e / (1.0 + e)
    oh = jnp.where(oh1 | oh2, 1.0, 0.0)
    r = lax.broadcasted_iota(jnp.int32, (tm, tm), 0)
    c = lax.broadcasted_iota(jnp.int32, (tm, tm), 1)
    lower = jnp.where(c < r, 1.0, 0.0).astype(BF16)
    before = jnp.dot(lower, oh.astype(BF16), preferred_element_type=F32) + base_scr[...]
    rank1 = jnp.sum(jnp.where(oh1, before, 0.0), axis=-1, keepdims=True)
    rank2 = jnp.sum(jnp.where(oh2, before, 0.0), axis=-1, keepdims=True)
    base_scr[...] += jnp.sum(oh, axis=0, keepdims=True)
    meta = jnp.zeros((tm, LANES), F32)
    for idx, val in ((META_E1, i1.astype(F32)), (META_E2, i2.astype(F32)),
                     (META_R1, rank1), (META_R2, rank2), (META_G1, g1), (META_G2, g2)):
        meta = jnp.where(lane == idx, val, meta)
    meta_ref[...] = meta
    meta_t_ref[...] = jnp.transpose(meta)[:SUBLANES]
    cnt_ref[...] = base_scr[...]


def _router(x2, w_router):
    T = x2.shape[0]
    tm = min(ROUTER_TM, T)
    w = jnp.zeros((D_MODEL, LANES), F32).at[:, :N_EXPERTS].set(w_router.astype(F32))
    return pl.pallas_call(
        _router_kernel,
        grid=(T // tm,),
        in_specs=[pl.BlockSpec((tm, D_MODEL), lambda i: (i, 0)),
                  pl.BlockSpec((D_MODEL, LANES), lambda i: (0, 0))],
        out_specs=[pl.BlockSpec((tm, LANES), lambda i: (i, 0)),
                   pl.BlockSpec((SUBLANES, tm), lambda i: (0, i)),
                   pl.BlockSpec((1, LANES), lambda i: (0, 0))],
        out_shape=[jax.ShapeDtypeStruct((T, LANES), F32),
                   jax.ShapeDtypeStruct((SUBLANES, T), F32),
                   jax.ShapeDtypeStruct((1, LANES), F32)],
        scratch_shapes=[pltpu.VMEM((1, LANES), F32)],
        compiler_params=_params("arbitrary"),
    )(x2, w)


DISP_TS = 256
DISP_ZB = 128


def _row_copy(src, src_row8, dst, dst_row8, sem):
    return pltpu.make_async_copy(
        src.at[pl.ds(pl.multiple_of(src_row8, SUBLANES), SEGS), :],
        dst.at[pl.ds(pl.multiple_of(dst_row8, SUBLANES), SEGS), :], sem)


def _dispatch_kernel(pad0_ref, padn_ref, tail_ref, p1_ref, p2_ref, x_ref, xs_ref,
                     sbuf, zbuf, sem, zsem):
    i = pl.program_id(0)
    ts = x_ref.shape[0]
    slot = i % 2
    stage = sbuf.at[slot]
    for s in range(SEGS):
        stage[pl.ds(s, ts, stride=SEGS), :] = x_ref[:, s * LANES:(s + 1) * LANES]

    def start(r, carry):
        _row_copy(stage, r * SEGS, xs_ref, p1_ref[r], sem.at[slot]).start(priority=0)
        _row_copy(stage, r * SEGS, xs_ref, p2_ref[r], sem.at[slot]).start(priority=1)
        return carry

    def wait_step(which):
        for _ in range(2):
            pltpu.make_async_copy(sbuf.at[which], xs_ref.at[pl.ds(0, ts * SEGS), :],
                                  sem.at[which]).wait()

    lax.fori_loop(0, ts, start, 0)

    @pl.when(pl.program_id(0) == 0)
    def _():
        zbuf[...] = jnp.zeros_like(zbuf)
        for e in range(N_EXPERTS):
            def zstart(j, carry, e=e):
                _row_copy(zbuf, 0, xs_ref, pad0_ref[e] + j * SEGS, zsem).start()
                return carry

            def zwait(j, carry, e=e):
                _row_copy(zbuf, 0, xs_ref, pad0_ref[e] + j * SEGS, zsem).wait()
                return carry

            lax.fori_loop(0, padn_ref[e], zstart, 0)
            lax.fori_loop(0, padn_ref[e], zwait, 0)

        def tail_copy(j):
            rows = DISP_ZB * SEGS
            dst0 = pl.multiple_of(tail_ref[0] + j * rows, SUBLANES)
            return pltpu.make_async_copy(zbuf, xs_ref.at[pl.ds(dst0, rows), :], zsem)

        def tstart(j, carry):
            tail_copy(j).start()
            return carry

        def twait(j, carry):
            tail_copy(j).wait()
            return carry

        lax.fori_loop(0, tail_ref[1], tstart, 0)
        lax.fori_loop(0, tail_ref[1], twait, 0)

    @pl.when(i > 0)
    def _():
        wait_step(1 - slot)

    @pl.when(i == pl.num_programs(0) - 1)
    def _():
        wait_step(slot)


def _dispatch(x2, pos1_8, pos2_8, pad0_8, padn, tail, n_slots):
    T = x2.shape[0]
    ts = min(DISP_TS, T)
    smem = lambda: pl.BlockSpec((ts,), lambda i, *_: (i,), memory_space=pltpu.SMEM)
    return pl.pallas_call(
        _dispatch_kernel,
        grid_spec=pltpu.PrefetchScalarGridSpec(
            num_scalar_prefetch=3,
            grid=(T // ts,),
            in_specs=[smem(), smem(),
                      pl.BlockSpec((ts, D_MODEL), lambda i, *_: (i, 0))],
            out_specs=pl.BlockSpec(memory_space=pl.ANY),
            scratch_shapes=[pltpu.VMEM((2, ts * SEGS, LANES), F32),
                            pltpu.VMEM((DISP_ZB * SEGS, LANES), F32),
                            pltpu.SemaphoreType.DMA((2,)),
                            pltpu.SemaphoreType.DMA(())]),
        out_shape=jax.ShapeDtypeStruct((n_slots * SEGS, LANES), F32),
        compiler_params=_params("arbitrary"),
    )(pad0_8, padn, tail, pos1_8, pos2_8, x2)


MOE_TM = 1024
MOE_FC = 512


def _moe_kernel(te_ref, nu_ref, xs_ref, wg_ref, wu_ref, wd_ref, o_ref, xb_scr, acc_scr):
    i = pl.program_id(0)
    k = pl.program_id(1)
    tm = xb_scr.shape[0]

    @pl.when(i < nu_ref[0])
    def _():
        @pl.when(k == 0)
        def _():
            for s in range(SEGS):
                xb_scr[:, s * LANES:(s + 1) * LANES] = (
                    xs_ref[pl.ds(s, tm, stride=SEGS), :].astype(BF16))
            acc_scr[...] = jnp.zeros_like(acc_scr)

        acc_scr[...] += _swiglu_step(xb_scr[...], wg_ref, wu_ref, wd_ref)

        @pl.when(k == pl.num_programs(1) - 1)
        def _():
            for s in range(SEGS):
                o_ref[pl.ds(s, tm, stride=SEGS), :] = acc_scr[:, s * LANES:(s + 1) * LANES]

    @pl.when((i >= nu_ref[0]) & (k == 0))
    def _():
        o_ref[...] = jnp.zeros_like(o_ref)


def _moe(xs, tile_expert, n_used, wg, wu, wd, tm):
    n_tiles = xs.shape[0] // (tm * SEGS)
    fc = MOE_FC
    nk = D_FF // fc

    def tile(i, k, te, nu):
        return (jnp.minimum(i, nu[0] - 1), 0)

    def kk(i, k, nu):
        return jnp.where(i < nu[0], k, nk - 1)

    return pl.pallas_call(
        _moe_kernel,
        grid_spec=pltpu.PrefetchScalarGridSpec(
            num_scalar_prefetch=2,
            grid=(n_tiles, nk),
            in_specs=[pl.BlockSpec((tm * SEGS, LANES), tile),
                      pl.BlockSpec((None, D_MODEL, fc),
                                   lambda i, k, te, nu: (te[i], 0, kk(i, k, nu))),
                      pl.BlockSpec((None, D_MODEL, fc),
                                   lambda i, k, te, nu: (te[i], 0, kk(i, k, nu))),
                      pl.BlockSpec((None, fc, D_MODEL),
                                   lambda i, k, te, nu: (te[i], kk(i, k, nu), 0))],
            out_specs=pl.BlockSpec((tm * SEGS, LANES), lambda i, k, te, nu: (i, 0)),
            scratch_shapes=[pltpu.VMEM((tm, D_MODEL), BF16), pltpu.VMEM((tm, D_MODEL), F32)]),
        out_shape=jax.ShapeDtypeStruct(xs.shape, F32),
        compiler_params=_params("arbitrary", "arbitrary"),
    )(tile_expert, n_used, xs, wg, wu, wd)


COMB_TS = 256


def _combine_kernel(p1_ref, p2_ref, p1n_ref, p2n_ref, ys_ref, meta_ref, x_ref, g_ref, b_ref,
                    o_ref, buf1, buf2, sem):
    i = pl.program_id(0)
    ts = x_ref.shape[0]
    slot = i % 2

    def gather(pa_ref, pb_ref, which):
        def start(r, carry):
            _row_copy(ys_ref, pa_ref[r], buf1.at[which], r * SEGS,
                      sem.at[which]).start(priority=0)
            _row_copy(ys_ref, pb_ref[r], buf2.at[which], r * SEGS,
                      sem.at[which]).start(priority=1)
            return carry
        lax.fori_loop(0, ts, start, 0)

    @pl.when(i == 0)
    def _():
        gather(p1_ref, p2_ref, slot)

    @pl.when(i < pl.num_programs(0) - 1)
    def _():
        gather(p1n_ref, p2n_ref, 1 - slot)

    for buf in (buf1, buf2):
        pltpu.make_async_copy(ys_ref.at[pl.ds(0, ts * SEGS), :], buf.at[slot],
                              sem.at[slot]).wait()
    meta = meta_ref[...]
    g1 = meta[:, META_G1:META_G1 + 1]
    g2 = meta[:, META_G2:META_G2 + 1]
    rows1 = buf1.at[slot]
    rows2 = buf2.at[slot]
    for s in range(SEGS):
        cols = slice(s * LANES, (s + 1) * LANES)
        y = (g1 * rows1[pl.ds(s, ts, stride=SEGS), :]
             + g2 * rows2[pl.ds(s, ts, stride=SEGS), :])
        o_ref[:, cols] = DN_ALPHA * x_ref[:, cols] + y
    o_ref[...] = _layer_norm(o_ref[...], g_ref[...], b_ref[...])


def _combine(ys, pos1_8, pos2_8, meta, x2, g, b):
    T = x2.shape[0]
    ts = min(COMB_TS, T)
    n = T // ts
    smem = lambda: pl.BlockSpec((ts,), lambda i: (i,), memory_space=pltpu.SMEM)
    smem_next = lambda: pl.BlockSpec((ts,), lambda i: (jnp.minimum(i + 1, n - 1),),
                                     memory_space=pltpu.SMEM)
    vec = pl.BlockSpec((1, D_MODEL), lambda i: (0, 0))
    return pl.pallas_call(
        _combine_kernel,
        grid=(n,),
        in_specs=[smem(), smem(), smem_next(), smem_next(),
                  pl.BlockSpec(memory_space=pl.ANY),
                  pl.BlockSpec((ts, LANES), lambda i: (i, 0)),
                  pl.BlockSpec((ts, D_MODEL), lambda i: (i, 0)),
                  vec, vec],
        out_specs=pl.BlockSpec((ts, D_MODEL), lambda i: (i, 0)),
        out_shape=jax.ShapeDtypeStruct((T, D_MODEL), F32),
        scratch_shapes=[pltpu.VMEM((2, ts * SEGS, LANES), F32),
                        pltpu.VMEM((2, ts * SEGS, LANES), F32),
                        pltpu.SemaphoreType.DMA((2,))],
        compiler_params=_params("arbitrary"),
    )(pos1_8, pos2_8, pos1_8, pos2_8, ys, meta, x2, g.reshape(1, -1), b.reshape(1, -1))


def _moe_block(x2, w_router, wg, wu, wd, g, b):
    T = x2.shape[0]
    tm = min(MOE_TM, T)
    meta, meta_t, cnt = _router(x2, w_router)
    field = lambda idx: meta_t[idx].astype(jnp.int32)
    e1, e2, r1, r2 = field(META_E1), field(META_E2), field(META_R1), field(META_R2)
    counts = cnt[0, :N_EXPERTS].astype(jnp.int32)
    tiles_e = (counts + tm - 1) // tm
    padded = tiles_e * tm
    starts = jnp.cumsum(padded) - padded
    tile_end = jnp.cumsum(tiles_e)
    n_used = tile_end[-1]
    n_tiles = (TOPK_SLOTS * T) // tm + N_EXPERTS
    tid = jnp.minimum(jnp.arange(n_tiles, dtype=jnp.int32), n_used - 1)
    tile_expert = jnp.sum((tid[:, None] >= tile_end[None, :]).astype(jnp.int32), axis=1)
    experts = jnp.arange(N_EXPERTS, dtype=jnp.int32)[:, None]
    start_of = lambda e: jnp.sum(jnp.where(e[None, :] == experts, starts[:, None], 0), axis=0)
    pos1_8 = (start_of(e1) + r1) * SEGS
    pos2_8 = (start_of(e2) + r2) * SEGS
    pad0_8 = (starts + counts) * SEGS
    padn = padded - counts
    tail = jnp.stack([n_used * (tm * SEGS), (n_tiles - n_used) * (tm // DISP_ZB)])
    xs = _dispatch(x2, pos1_8, pos2_8, pad0_8, padn, tail, n_tiles * tm)
    ys = _moe(xs, tile_expert, n_used.reshape(1), wg, wu, wd, tm)
    return _combine(ys, pos1_8, pos2_8, meta, x2, g, b)


TOPK_SLOTS = 2


def kernel(x, attn_w_qkv, attn_sink, attn_w_o, conv_pw1_w, conv_pw1_b, conv_dw_w, conv_dw_b,
           conv_norm_g, conv_norm_b, conv_pw2_w, conv_pw2_b, ffn_w_gate, ffn_w_up, ffn_w_down,
           moe_router, moe_w_gate, moe_w_up, moe_w_down, ln_mix_g, ln_mix_b, ln_ffn_g, ln_ffn_b):
    batch, seq, _ = x.shape
    x2 = x.reshape(batch * seq, D_MODEL)

    q, k, v = _qkv_rope(x2, attn_w_qkv[0].astype(BF16), seq)
    att = _attention(q, k, v, attn_sink[0], batch, seq)
    x2 = _attn_out_ffn(att, x2, attn_w_o[0].astype(BF16), ln_mix_g[0], ln_mix_b[0],
                       ffn_w_gate[0], ffn_w_up[0], ffn_w_down[0], ln_ffn_g[0], ln_ffn_b[0])

    h = _pw1_glu(x2, conv_pw1_w[0].astype(BF16), conv_pw1_b[0])
    x2 = _conv_block(h, x2, conv_dw_w[0], conv_dw_b[0], conv_norm_g[0], conv_norm_b[0],
                     conv_pw2_w[0].astype(BF16), conv_pw2_b[0], ln_mix_g[1], ln_mix_b[1],
                     batch, seq)
    x2 = _moe_block(x2, moe_router[0], moe_w_gate[0], moe_w_up[0], moe_w_down[0],
                    ln_ffn_g[1], ln_ffn_b[1])
    return x2.reshape(batch, seq, D_MODEL)
```

```python
from typing import NamedTuple

import jax
import jax.numpy as jnp
from jax import lax
from jax.experimental import pallas as pl
from jax.experimental.pallas import tpu as pltpu

F32 = jnp.float32
BF16 = jnp.bfloat16

D_MODEL = 1024
N_HEADS = 16
N_KV_HEADS = 4
HEAD_DIM = D_MODEL // N_HEADS
GROUP = N_HEADS // N_KV_HEADS
Q_DIM = N_HEADS * HEAD_DIM
KV_DIM = N_KV_HEADS * HEAD_DIM
WINDOW = 128
BLOCK = 128
ROPE_THETA = 10000.0
CONV_WIDTH = 31
CONV_HALF = CONV_WIDTH // 2
D_FF = 3584
N_EXPERTS = 8
LN_EPS = 1e-5
DEPTH = 2
DN_ALPHA = (2 * DEPTH) ** 0.25
NEG_INF = -1e30

LOG2E = 1.4426950408889634
LANES = 128
SUBLANES = 8
SEGS = D_MODEL // LANES
VAUG_DIM = N_KV_HEADS * LANES
VMEM_LIMIT = 56 * 1024 * 1024


def _params(*sem):
    return pltpu.CompilerParams(dimension_semantics=sem, vmem_limit_bytes=VMEM_LIMIT)


def _layer_norm(y, g, b):
    mu = jnp.mean(y, axis=-1, keepdims=True)
    d = y - mu
    var = jnp.mean(d * d, axis=-1, keepdims=True)
    return d * lax.rsqrt(var + LN_EPS) * g + b


def _silu(v):
    return v * jax.nn.sigmoid(v)


def _qkv_kernel(x_ref, w_ref, cos_ref, sa_ref, sb_ref, q_ref, k_ref, v_ref):
    xb = x_ref[...].astype(BF16)
    acc = jnp.dot(xb, w_ref[...], preferred_element_type=F32)
    cos = cos_ref[...]
    sa = sa_ref[...]
    sb = sb_ref[...]

    def rope(t):
        return (t * cos + pltpu.roll(t, LANES - HEAD_DIM // 2, 1) * sa
                + pltpu.roll(t, HEAD_DIM // 2, 1) * sb)

    scale = HEAD_DIM ** -0.5 * LOG2E
    for c in range(Q_DIM // LANES):
        t = acc[:, c * LANES:(c + 1) * LANES]
        q_ref[:, c * LANES:(c + 1) * LANES] = (rope(t) * scale).astype(BF16)
    for c in range(KV_DIM // LANES):
        t = acc[:, Q_DIM + c * LANES:Q_DIM + (c + 1) * LANES]
        k_ref[:, c * LANES:(c + 1) * LANES] = rope(t).astype(BF16)
    ones = jnp.ones((acc.shape[0], HEAD_DIM), F32)
    for g in range(N_KV_HEADS):
        vg = acc[:, Q_DIM + KV_DIM + g * HEAD_DIM:Q_DIM + KV_DIM + (g + 1) * HEAD_DIM]
        v_ref[:, g * LANES:(g + 1) * LANES] = jnp.concatenate([vg, ones], axis=1).astype(BF16)


def _qkv_rope(x2, w_qkv, seq, tm=512):
    T = x2.shape[0]
    pos = jnp.arange(seq, dtype=F32)
    inv_freq = ROPE_THETA ** (-jnp.arange(0, HEAD_DIM, 2, dtype=F32) / HEAD_DIM)
    ang = pos[:, None] * inv_freq[None, :]
    cos_h, sin_h = jnp.cos(ang), jnp.sin(ang)
    zero = jnp.zeros_like(sin_h)
    reps = LANES // HEAD_DIM
    cos = jnp.tile(jnp.concatenate([cos_h, cos_h], -1), (1, reps))
    sa = jnp.tile(jnp.concatenate([-sin_h, zero], -1), (1, reps))
    sb = jnp.tile(jnp.concatenate([zero, sin_h], -1), (1, reps))
    nseq = seq // tm
    tab = pl.BlockSpec((tm, LANES), lambda i: (i % nseq, 0))
    n_out = Q_DIM + 2 * KV_DIM
    return pl.pallas_call(
        _qkv_kernel,
        grid=(T // tm,),
        in_specs=[pl.BlockSpec((tm, D_MODEL), lambda i: (i, 0)),
                  pl.BlockSpec((D_MODEL, n_out), lambda i: (0, 0)),
                  tab, tab, tab],
        out_specs=[pl.BlockSpec((tm, Q_DIM), lambda i: (i, 0)),
                   pl.BlockSpec((tm, KV_DIM), lambda i: (i, 0)),
                   pl.BlockSpec((tm, VAUG_DIM), lambda i: (i, 0))],
        out_shape=[jax.ShapeDtypeStruct((T, Q_DIM), BF16),
                   jax.ShapeDtypeStruct((T, KV_DIM), BF16),
                   jax.ShapeDtypeStruct((T, VAUG_DIM), BF16)],
        compiler_params=_params("parallel"),
    )(x2, w_qkv, cos, sa, sb)


ATT_TQ = 512
ATT_SUB = ATT_TQ // BLOCK
ATT_KEYS = 3 * BLOCK
ATT_ROWS = GROUP * BLOCK


def _attn_kernel(sink_ref, q_ref, kp_ref, km_ref, kn_ref, vp_ref, vm_ref, vn_ref,
                 o_ref, kbuf, vbuf):
    i = pl.program_id(1)
    last = pl.num_programs(1) - 1
    kbuf[0:BLOCK] = kp_ref[...]
    kbuf[BLOCK:BLOCK + ATT_TQ] = km_ref[...]
    kbuf[BLOCK + ATT_TQ:] = kn_ref[...]
    vbuf[0:BLOCK] = vp_ref[...]
    vbuf[BLOCK:BLOCK + ATT_TQ] = vm_ref[...]
    vbuf[BLOCK + ATT_TQ:] = vn_ref[...]

    row = lax.broadcasted_iota(jnp.int32, (ATT_ROWS, BLOCK), 0)
    key = lax.broadcasted_iota(jnp.int32, (ATT_ROWS, BLOCK), 1)
    ql = row & (BLOCK - 1)
    band_prev = key >= ql
    band_next = key <= ql
    hrow = lax.broadcasted_iota(jnp.int32, (ATT_ROWS, 1), 0) // BLOCK

    def block(j, carry):
        r0 = pl.multiple_of(j * BLOCK, BLOCK)
        has_prev = jnp.logical_not((i == 0) & (j == 0))
        has_next = jnp.logical_not((i == last) & (j == ATT_SUB - 1))
        valid_prev = band_prev & has_prev
        valid_next = band_next & has_next
        for g in range(N_KV_HEADS):
            qb = q_ref[pl.ds(r0, BLOCK), g * GROUP * HEAD_DIM:(g + 1) * GROUP * HEAD_DIM]
            q4 = jnp.concatenate(
                [qb[:, h * HEAD_DIM:(h + 1) * HEAD_DIM] for h in range(GROUP)], axis=0)
            kb = kbuf[pl.ds(r0, ATT_KEYS), g * HEAD_DIM:(g + 1) * HEAD_DIM]
            vb = vbuf[pl.ds(r0, ATT_KEYS), g * LANES:(g + 1) * LANES]
            s = lax.dot_general(q4, kb, (((1,), (1,)), ((), ())),
                                preferred_element_type=F32)
            s = jnp.concatenate(
                [jnp.where(valid_prev, s[:, :BLOCK], NEG_INF), s[:, BLOCK:2 * BLOCK],
                 jnp.where(valid_next, s[:, 2 * BLOCK:], NEG_INF)], axis=1)
            sk = jnp.zeros((ATT_ROWS, 1), F32)
            for h in range(GROUP):
                sk = jnp.where(hrow == h, sink_ref[g * GROUP + h] * LOG2E, sk)
            m = jnp.maximum(jnp.max(s, axis=-1, keepdims=True), sk)
            p = jnp.exp2(s - m)
            oa = jnp.dot(p.astype(BF16), vb, preferred_element_type=F32)
            den = oa[:, HEAD_DIM:] + jnp.exp2(sk - m)
            o = oa[:, :HEAD_DIM] / den
            o_ref[pl.ds(r0, BLOCK), g * GROUP * HEAD_DIM:(g + 1) * GROUP * HEAD_DIM] = (
                jnp.concatenate([o[h * BLOCK:(h + 1) * BLOCK] for h in range(GROUP)],
                                axis=1).astype(BF16))
        return carry

    lax.fori_loop(0, ATT_SUB, block, 0)


def _attention(q, k, v, sink, batch, seq):
    nb = seq // BLOCK
    q3 = q.reshape(batch, seq, Q_DIM)
    k3 = k.reshape(batch, seq, KV_DIM)
    v3 = v.reshape(batch, seq, VAUG_DIM)
    prev = lambda w: pl.BlockSpec((None, BLOCK, w),
                                  lambda b, i, s: (b, jnp.maximum(i * ATT_SUB - 1, 0), 0))
    main = lambda w: pl.BlockSpec((None, ATT_TQ, w), lambda b, i, s: (b, i, 0))
    nxt = lambda w: pl.BlockSpec(
        (None, BLOCK, w), lambda b, i, s: (b, jnp.minimum((i + 1) * ATT_SUB, nb - 1), 0))
    out = pl.pallas_call(
        _attn_kernel,
        grid_spec=pltpu.PrefetchScalarGridSpec(
            num_scalar_prefetch=1,
            grid=(batch, seq // ATT_TQ),
            in_specs=[pl.BlockSpec((None, ATT_TQ, Q_DIM), lambda b, i, s: (b, i, 0)),
                      prev(KV_DIM), main(KV_DIM), nxt(KV_DIM),
                      prev(VAUG_DIM), main(VAUG_DIM), nxt(VAUG_DIM)],
            out_specs=pl.BlockSpec((None, ATT_TQ, Q_DIM), lambda b, i, s: (b, i, 0)),
            scratch_shapes=[pltpu.VMEM((ATT_TQ + 2 * BLOCK, KV_DIM), BF16),
                            pltpu.VMEM((ATT_TQ + 2 * BLOCK, VAUG_DIM), BF16)]),
        out_shape=jax.ShapeDtypeStruct((batch, seq, Q_DIM), BF16),
        compiler_params=_params("parallel", "parallel"),
    )(sink.astype(F32), q3, k3, k3, k3, v3, v3, v3)
    return out.reshape(batch * seq, Q_DIM)


FFN_FC = 512
FFN_TM = 1024


def _swiglu_step(xb, wg_ref, wu_ref, wd_ref):
    gate = jnp.dot(xb, wg_ref[...].astype(BF16), preferred_element_type=F32)
    up = jnp.dot(xb, wu_ref[...].astype(BF16), preferred_element_type=F32)
    h = (_silu(gate) * up).astype(BF16)
    return jnp.dot(h, wd_ref[...].astype(BF16), preferred_element_type=F32)


def _attn_out_ffn_kernel(att_ref, x_ref, wo_ref, g1_ref, b1_ref, wg_ref, wu_ref, wd_ref,
                         g2_ref, b2_ref, o_ref, x1_scr, xb_scr, acc_scr):
    k = pl.program_id(1)

    @pl.when(k == 0)
    def _():
        y = jnp.dot(att_ref[...], wo_ref[...], preferred_element_type=F32)
        x1 = _layer_norm(DN_ALPHA * x_ref[...] + y, g1_ref[...], b1_ref[...])
        x1_scr[...] = x1
        xb_scr[...] = x1.astype(BF16)
        acc_scr[...] = jnp.zeros_like(acc_scr)

    acc_scr[...] += _swiglu_step(xb_scr[...], wg_ref, wu_ref, wd_ref)

    @pl.when(k == pl.num_programs(1) - 1)
    def _():
        y = DN_ALPHA * x1_scr[...] + acc_scr[...]
        o_ref[...] = _layer_norm(y, g2_ref[...], b2_ref[...])


def _attn_out_ffn(att, x2, wo, g1, b1, wg, wu, wd, g2, b2):
    T = x2.shape[0]
    tm, fc = min(FFN_TM, T), FFN_FC
    vec = pl.BlockSpec((1, D_MODEL), lambda i, k: (0, 0))
    row = pl.BlockSpec((tm, D_MODEL), lambda i, k: (i, 0))
    return pl.pallas_call(
        _attn_out_ffn_kernel,
        grid=(T // tm, D_FF // fc),
        in_specs=[row, row,
                  pl.BlockSpec((D_MODEL, D_MODEL), lambda i, k: (0, 0)),
                  vec, vec,
                  pl.BlockSpec((D_MODEL, fc), lambda i, k: (0, k)),
                  pl.BlockSpec((D_MODEL, fc), lambda i, k: (0, k)),
                  pl.BlockSpec((fc, D_MODEL), lambda i, k: (k, 0)),
                  vec, vec],
        out_specs=row,
        out_shape=jax.ShapeDtypeStruct((T, D_MODEL), F32),
        scratch_shapes=[pltpu.VMEM((tm, D_MODEL), F32), pltpu.VMEM((tm, D_MODEL), BF16),
                        pltpu.VMEM((tm, D_MODEL), F32)],
        compiler_params=_params("parallel", "arbitrary"),
    )(att, x2, wo, g1.reshape(1, -1), b1.reshape(1, -1), wg, wu, wd,
      g2.reshape(1, -1), b2.reshape(1, -1))


def _pw1_glu_kernel(x_ref, w_ref, bias_ref, o_ref):
    h = jnp.dot(x_ref[...].astype(BF16), w_ref[...], preferred_element_type=F32) + bias_ref[...]
    o_ref[...] = h[:, :D_MODEL] * jax.nn.sigmoid(h[:, D_MODEL:])


def _pw1_glu(x2, w, bias, tm=512):
    T = x2.shape[0]
    return pl.pallas_call(
        _pw1_glu_kernel,
        grid=(T // tm,),
        in_specs=[pl.BlockSpec((tm, D_MODEL), lambda i: (i, 0)),
                  pl.BlockSpec((D_MODEL, 2 * D_MODEL), lambda i: (0, 0)),
                  pl.BlockSpec((1, 2 * D_MODEL), lambda i: (0, 0))],
        out_specs=pl.BlockSpec((tm, D_MODEL), lambda i: (i, 0)),
        out_shape=jax.ShapeDtypeStruct((T, D_MODEL), F32),
        compiler_params=_params("parallel"),
    )(x2, w, bias.reshape(1, -1))


CONV_TM = 512
CONV_HALO = 16
CONV_RC = 64
CONV_PARTS = 2
CONV_SHIFT_ROWS = CONV_TM + 2 * CONV_HALO - SUBLANES


def _conv_kernel(hp_ref, hm_ref, hn_ref, x_ref, dw_ref, dwb_ref, ng_ref, nb_ref,
                 w2_ref, b2_ref, g_ref, b_ref, o_ref, hbuf, hshift, cbuf):
    i = pl.program_id(1)
    last = pl.num_programs(1) - 1
    tm = CONV_TM
    hbuf[0:CONV_HALO] = jnp.where(i > 0, hp_ref[...], 0.0)
    hbuf[CONV_HALO:CONV_HALO + tm] = hm_ref[...]
    hbuf[CONV_HALO + tm:] = jnp.where(i < last, hn_ref[...], 0.0)
    for sh in range(1, SUBLANES):
        hshift[sh - 1] = hbuf[sh:sh + CONV_SHIFT_ROWS, :]
    off = CONV_HALO - CONV_HALF

    def rows(r, carry):
        r0 = pl.multiple_of(r * CONV_RC, CONV_RC)
        for c in range(SEGS):
            cols = slice(c * LANES, (c + 1) * LANES)
            parts = [None] * CONV_PARTS
            for sh in range(SUBLANES):
                taps = [w for w in range(CONV_WIDTH) if (off + w) % SUBLANES == sh]
                src = hbuf if sh == 0 else hshift.at[sh - 1]
                span = (off + taps[-1]) // SUBLANES * SUBLANES + CONV_RC
                seg = src[pl.ds(r0, span), cols]
                for w in taps:
                    lo = (off + w) // SUBLANES * SUBLANES
                    term = seg[lo:lo + CONV_RC] * dw_ref[w:w + 1, cols]
                    k = w % CONV_PARTS
                    parts[k] = term if parts[k] is None else parts[k] + term
            acc = parts[0]
            for part in parts[1:]:
                acc = acc + part
            cbuf[pl.ds(r0, CONV_RC), cols] = acc + dwb_ref[:, cols]
        return carry

    lax.fori_loop(0, tm // CONV_RC, rows, 0)
    h = _silu(_layer_norm(cbuf[...], ng_ref[...], nb_ref[...]))
    y = jnp.dot(h.astype(BF16), w2_ref[...], preferred_element_type=F32) + b2_ref[...]
    y = DN_ALPHA * x_ref[...] + y
    o_ref[...] = _layer_norm(y, g_ref[...], b_ref[...])


def _conv_block(h, x2, dw_w, dw_b, ng, nb, w2, b2, g, b, batch, seq):
    tm = CONV_TM
    h3 = h.reshape(batch, seq, D_MODEL)
    x3 = x2.reshape(batch, seq, D_MODEL)
    per = tm // CONV_HALO
    nh = seq // CONV_HALO
    vec = pl.BlockSpec((1, D_MODEL), lambda bb, i: (0, 0))
    out = pl.pallas_call(
        _conv_kernel,
        grid=(batch, seq // tm),
        in_specs=[pl.BlockSpec((None, CONV_HALO, D_MODEL),
                               lambda bb, i: (bb, jnp.maximum(i * per - 1, 0), 0)),
                  pl.BlockSpec((None, tm, D_MODEL), lambda bb, i: (bb, i, 0)),
                  pl.BlockSpec((None, CONV_HALO, D_MODEL),
                               lambda bb, i: (bb, jnp.minimum((i + 1) * per, nh - 1), 0)),
                  pl.BlockSpec((None, tm, D_MODEL), lambda bb, i: (bb, i, 0)),
                  pl.BlockSpec((CONV_WIDTH, D_MODEL), lambda bb, i: (0, 0)),
                  vec, vec, vec,
                  pl.BlockSpec((D_MODEL, D_MODEL), lambda bb, i: (0, 0)),
                  vec, vec, vec],
        out_specs=pl.BlockSpec((None, tm, D_MODEL), lambda bb, i: (bb, i, 0)),
        out_shape=jax.ShapeDtypeStruct((batch, seq, D_MODEL), F32),
        scratch_shapes=[pltpu.VMEM((tm + 2 * CONV_HALO, D_MODEL), F32),
                        pltpu.VMEM((SUBLANES - 1, CONV_SHIFT_ROWS, D_MODEL), F32),
                        pltpu.VMEM((tm, D_MODEL), F32)],
        compiler_params=_params("parallel", "parallel"),
    )(h3, h3, h3, x3, dw_w, dw_b.reshape(1, -1), ng.reshape(1, -1), nb.reshape(1, -1),
      w2, b2.reshape(1, -1), g.reshape(1, -1), b.reshape(1, -1))
    return out.reshape(batch * seq, D_MODEL)


ROUTE_TS = 512
ROUTE_BITS = ROUTE_TS.bit_length()
META_E1, META_E2, META_R1, META_R2, META_G1, META_G2 = range(6)


def _router_kernel(x_ref, w_ref, meta_t_ref, base_ref, cnt_ref, base_scr):
    t = pl.program_id(0)

    @pl.when(t == 0)
    def _():
        base_scr[...] = jnp.zeros_like(base_scr)

    base_ref[...] = base_scr[...]

    tm = x_ref.shape[0]
    x = x_ref[...]
    w = w_ref[...]
    xh = x.astype(BF16)
    xl = (x - xh.astype(F32)).astype(BF16)
    wh = w.astype(BF16)
    wl = (w - wh.astype(F32)).astype(BF16)
    logits = (jnp.dot(xh, wh, preferred_element_type=F32)
              + (jnp.dot(xh, wl, preferred_element_type=F32)
                 + jnp.dot(xl, wh, preferred_element_type=F32)))
    lane = lax.broadcasted_iota(jnp.int32, (tm, LANES), 1)
    lg = jnp.where(lane < N_EXPERTS, logits, -jnp.inf)
    m1 = jnp.max(lg, axis=-1, keepdims=True)
    i1 = jnp.min(jnp.where(lg == m1, lane, LANES), axis=-1, keepdims=True)
    oh1 = lane == i1
    lg2 = jnp.where(oh1, -jnp.inf, lg)
    m2 = jnp.max(lg2, axis=-1, keepdims=True)
    i2 = jnp.min(jnp.where(lg2 == m2, lane, LANES), axis=-1, keepdims=True)
    oh2 = lane == i2
    e = jnp.exp(m2 - m1)
    g1 = 1.0 / (1.0 + e)
    g2 = e / (1.0 + e)
    oh = jnp.where(oh1 | oh2, 1.0, 0.0)
    r = lax.broadcasted_iota(jnp.int32, (tm, tm), 0)
    c = lax.broadcasted_iota(jnp.int32, (tm, tm), 1)
    lower = jnp.where(c < r, 1.0, 0.0).astype(BF16)
    before = jnp.dot(lower, oh.astype(BF16), preferred_element_type=F32) + base_scr[...]
    rank1 = jnp.sum(jnp.where(oh1, before, 0.0), axis=-1, keepdims=True)
    rank2 = jnp.sum(jnp.where(oh2, before, 0.0), axis=-1, keepdims=True)
    base_scr[...] += jnp.sum(oh, axis=0, keepdims=True)
    meta = jnp.zeros((tm, LANES), F32)
    for idx, val in ((META_E1, i1.astype(F32)), (META_E2, i2.astype(F32)),
                     (META_R1, rank1), (META_R2, rank2), (META_G1, g1), (META_G2, g2)):
        meta = jnp.where(lane == idx, val, meta)
    meta_t_ref[...] = jnp.transpose(meta)[:SUBLANES]
    cnt_ref[...] = base_scr[...]


def _router(x2, w_router):
    T = x2.shape[0]
    tm = min(ROUTE_TS, T)
    w = jnp.zeros((D_MODEL, LANES), F32).at[:, :N_EXPERTS].set(w_router.astype(F32))
    return pl.pallas_call(
        _router_kernel,
        grid=(T // tm,),
        in_specs=[pl.BlockSpec((tm, D_MODEL), lambda i: (i, 0)),
                  pl.BlockSpec((D_MODEL, LANES), lambda i: (0, 0))],
        out_specs=[pl.BlockSpec((SUBLANES, tm), lambda i: (0, i)),
                   pl.BlockSpec((None, 1, LANES), lambda i: (i, 0, 0)),
                   pl.BlockSpec((1, LANES), lambda i: (0, 0))],
        out_shape=[jax.ShapeDtypeStruct((SUBLANES, T), F32),
                   jax.ShapeDtypeStruct((T // tm, 1, LANES), F32),
                   jax.ShapeDtypeStruct((1, LANES), F32)],
        scratch_shapes=[pltpu.VMEM((1, LANES), F32)],
        compiler_params=_params("arbitrary"),
    )(x2, w)


DISP_ZB = 128


class _Runs(NamedTuple):
    slot8: jax.Array
    n: jax.Array
    off8: jax.Array


def _row_copy(src, src_row8, dst, dst_row8, sem):
    return pltpu.make_async_copy(
        src.at[pl.ds(pl.multiple_of(src_row8, SUBLANES), SEGS), :],
        dst.at[pl.ds(pl.multiple_of(dst_row8, SUBLANES), SEGS), :], sem)


def _run_copies(n, src, src0_8, dst, dst0_8, sem, priority):
    for bit in range(ROUTE_BITS):
        size = 1 << bit

        @pl.when((n & size) != 0)
        def _():
            above8 = ((n >> (bit + 1)) << (bit + 1)) * SEGS
            pltpu.make_async_copy(
                src.at[pl.ds(pl.multiple_of(src0_8 + above8, SUBLANES), size * SEGS), :],
                dst.at[pl.ds(pl.multiple_of(dst0_8 + above8, SUBLANES), size * SEGS), :],
                sem).start(priority=priority)


ROUTE_UNROLL = 8


def _dispatch_kernel(pad0_ref, padn_ref, tail_ref, run_dst_ref, run_n_ref, run_off_ref,
                     s1_ref, s2_ref, x_ref, xs_ref, sbuf, stage, zbuf, sem, zsem):
    i = pl.program_id(0)
    ts = x_ref.shape[0]
    slot = i % 2
    runs = stage.at[slot]
    for s in range(SEGS):
        sbuf[pl.ds(s, ts, stride=SEGS), :] = x_ref[:, s * LANES:(s + 1) * LANES]

    def place(c, carry):
        for u in range(ROUTE_UNROLL):
            r = c * ROUTE_UNROLL + u
            tile = sbuf[pl.ds(pl.multiple_of(r * SEGS, SUBLANES), SEGS), :]
            runs[pl.ds(pl.multiple_of(s1_ref[r], SUBLANES), SEGS), :] = tile
            runs[pl.ds(pl.multiple_of(s2_ref[r], SUBLANES), SEGS), :] = tile
        return carry

    lax.fori_loop(0, ts // ROUTE_UNROLL, place, 0)
    for e in range(N_EXPERTS):
        idx = i * N_EXPERTS + e
        _run_copies(run_n_ref[idx], runs, run_off_ref[idx], xs_ref, run_dst_ref[idx],
                    sem.at[slot], e % 2)

    def wait_step(which):
        pltpu.make_async_copy(stage.at[which], xs_ref.at[pl.ds(0, 2 * ts * SEGS), :],
                              sem.at[which]).wait()

    @pl.when(pl.program_id(0) == 0)
    def _():
        zbuf[...] = jnp.zeros_like(zbuf)
        for e in range(N_EXPERTS):
            def zstart(j, carry, e=e):
                _row_copy(zbuf, 0, xs_ref, pad0_ref[e] + j * SEGS, zsem).start()
                return carry

            def zwait(j, carry, e=e):
                _row_copy(zbuf, 0, xs_ref, pad0_ref[e] + j * SEGS, zsem).wait()
                return carry

            lax.fori_loop(0, padn_ref[e], zstart, 0)
            lax.fori_loop(0, padn_ref[e], zwait, 0)

        def tail_copy(j):
            rows = DISP_ZB * SEGS
            dst0 = pl.multiple_of(tail_ref[0] + j * rows, SUBLANES)
            return pltpu.make_async_copy(zbuf, xs_ref.at[pl.ds(dst0, rows), :], zsem)

        def tstart(j, carry):
            tail_copy(j).start()
            return carry

        def twait(j, carry):
            tail_copy(j).wait()
            return carry

        lax.fori_loop(0, tail_ref[1], tstart, 0)
        lax.fori_loop(0, tail_ref[1], twait, 0)

    @pl.when(i > 0)
    def _():
        wait_step(1 - slot)

    @pl.when(i == pl.num_programs(0) - 1)
    def _():
        wait_step(slot)


def _dispatch(x2, runs, sidx1_8, sidx2_8, pad0_8, padn, tail, n_slots):
    T = x2.shape[0]
    ts = min(ROUTE_TS, T)
    smem = lambda: pl.BlockSpec((ts,), lambda i, *_: (i,), memory_space=pltpu.SMEM)
    return pl.pallas_call(
        _dispatch_kernel,
        grid_spec=pltpu.PrefetchScalarGridSpec(
            num_scalar_prefetch=6,
            grid=(T // ts,),
            in_specs=[smem(), smem(),
                      pl.BlockSpec((ts, D_MODEL), lambda i, *_: (i, 0))],
            out_specs=pl.BlockSpec(memory_space=pl.ANY),
            scratch_shapes=[pltpu.VMEM((ts * SEGS, LANES), F32),
                            pltpu.VMEM((2, 2 * ts * SEGS, LANES), F32),
                            pltpu.VMEM((DISP_ZB * SEGS, LANES), F32),
                            pltpu.SemaphoreType.DMA((2,)),
                            pltpu.SemaphoreType.DMA(())]),
        out_shape=jax.ShapeDtypeStruct((n_slots * SEGS, LANES), F32),
        compiler_params=_params("arbitrary"),
    )(pad0_8, padn, tail, runs.slot8, runs.n, runs.off8, sidx1_8, sidx2_8, x2)


MOE_TM = 1024
MOE_FC = 512


def _moe_kernel(te_ref, nu_ref, xs_ref, wg_ref, wu_ref, wd_ref, o_ref, xb_scr, acc_scr):
    i = pl.program_id(0)
    k = pl.program_id(1)
    tm = xb_scr.shape[0]

    @pl.when(i < nu_ref[0])
    def _():
        @pl.when(k == 0)
        def _():
            for s in range(SEGS):
                xb_scr[:, s * LANES:(s + 1) * LANES] = (
                    xs_ref[pl.ds(s, tm, stride=SEGS), :].astype(BF16))
            acc_scr[...] = jnp.zeros_like(acc_scr)

        acc_scr[...] += _swiglu_step(xb_scr[...], wg_ref, wu_ref, wd_ref)

        @pl.when(k == pl.num_programs(1) - 1)
        def _():
            for s in range(SEGS):
                o_ref[pl.ds(s, tm, stride=SEGS), :] = acc_scr[:, s * LANES:(s + 1) * LANES]

    @pl.when((i >= nu_ref[0]) & (k == 0))
    def _():
        o_ref[...] = jnp.zeros_like(o_ref)


def _moe(xs, tile_expert, n_used, wg, wu, wd, tm):
    n_tiles = xs.shape[0] // (tm * SEGS)
    fc = MOE_FC
    nk = D_FF // fc

    def tile(i, k, te, nu):
        return (jnp.minimum(i, nu[0] - 1), 0)

    def kk(i, k, nu):
        return jnp.where(i < nu[0], k, nk - 1)

    return pl.pallas_call(
        _moe_kernel,
        grid_spec=pltpu.PrefetchScalarGridSpec(
            num_scalar_prefetch=2,
            grid=(n_tiles, nk),
            in_specs=[pl.BlockSpec((tm * SEGS, LANES), tile),
                      pl.BlockSpec((None, D_MODEL, fc),
                                   lambda i, k, te, nu: (te[i], 0, kk(i, k, nu))),
                      pl.BlockSpec((None, D_MODEL, fc),
                                   lambda i, k, te, nu: (te[i], 0, kk(i, k, nu))),
                      pl.BlockSpec((None, fc, D_MODEL),
                                   lambda i, k, te, nu: (te[i], kk(i, k, nu), 0))],
            out_specs=pl.BlockSpec((tm * SEGS, LANES), lambda i, k, te, nu: (i, 0)),
            scratch_shapes=[pltpu.VMEM((tm, D_MODEL), BF16), pltpu.VMEM((tm, D_MODEL), F32)]),
        out_shape=jax.ShapeDtypeStruct(xs.shape, F32),
        compiler_params=_params("arbitrary", "arbitrary"),
    )(tile_expert, n_used, xs, wg, wu, wd)


def _combine_kernel(run_src_ref, run_n_ref, run_off_ref, s1_ref, s2_ref, g1_ref, g2_ref,
                    ys_ref, x_ref, g_ref, b_ref, o_ref, stage, cbuf, sem):
    i = pl.program_id(0)
    ts = x_ref.shape[0]
    slot = i % 2

    def fetch(step, which):
        for e in range(N_EXPERTS):
            idx = step * N_EXPERTS + e
            _run_copies(run_n_ref[idx], ys_ref, run_src_ref[idx], stage.at[which],
                        run_off_ref[idx], sem.at[which], e % 2)

    @pl.when(i == 0)
    def _():
        fetch(i, slot)

    @pl.when(i < pl.num_programs(0) - 1)
    def _():
        fetch(i + 1, 1 - slot)

    pltpu.make_async_copy(ys_ref.at[pl.ds(0, 2 * ts * SEGS), :], stage.at[slot],
                          sem.at[slot]).wait()
    runs = stage.at[slot]

    def mix(c, carry):
        for u in range(ROUTE_UNROLL):
            r = c * ROUTE_UNROLL + u
            a = runs[pl.ds(pl.multiple_of(s1_ref[r], SUBLANES), SEGS), :]
            bb = runs[pl.ds(pl.multiple_of(s2_ref[r], SUBLANES), SEGS), :]
            cbuf[pl.ds(pl.multiple_of(r * SEGS, SUBLANES), SEGS), :] = (
                g1_ref[r] * a + g2_ref[r] * bb)
        return carry

    lax.fori_loop(0, ts // ROUTE_UNROLL, mix, 0)
    for s in range(SEGS):
        cols = slice(s * LANES, (s + 1) * LANES)
        o_ref[:, cols] = DN_ALPHA * x_ref[:, cols] + cbuf[pl.ds(s, ts, stride=SEGS), :]
    o_ref[...] = _layer_norm(o_ref[...], g_ref[...], b_ref[...])


def _combine(ys, runs, sidx1_8, sidx2_8, gate1, gate2, x2, g, b):
    T = x2.shape[0]
    ts = min(ROUTE_TS, T)
    smem = lambda: pl.BlockSpec((ts,), lambda i, *_: (i,), memory_space=pltpu.SMEM)
    vec = pl.BlockSpec((1, D_MODEL), lambda i, *_: (0, 0))
    return pl.pallas_call(
        _combine_kernel,
        grid_spec=pltpu.PrefetchScalarGridSpec(
            num_scalar_prefetch=3,
            grid=(T // ts,),
            in_specs=[smem(), smem(), smem(), smem(),
                      pl.BlockSpec(memory_space=pl.ANY),
                      pl.BlockSpec((ts, D_MODEL), lambda i, *_: (i, 0)),
                      vec, vec],
            out_specs=pl.BlockSpec((ts, D_MODEL), lambda i, *_: (i, 0)),
            scratch_shapes=[pltpu.VMEM((2, 2 * ts * SEGS, LANES), F32),
                            pltpu.VMEM((ts * SEGS, LANES), F32),
                            pltpu.SemaphoreType.DMA((2,))]),
        out_shape=jax.ShapeDtypeStruct((T, D_MODEL), F32),
        compiler_params=_params("arbitrary"),
    )(runs.slot8, runs.n, runs.off8, sidx1_8, sidx2_8, gate1, gate2, ys, x2,
      g.reshape(1, -1), b.reshape(1, -1))


def _moe_block(x2, w_router, wg, wu, wd, g, b):
    T = x2.shape[0]
    tm = min(MOE_TM, T)
    ts = min(ROUTE_TS, T)
    meta_t, base, cnt = _router(x2, w_router)
    field = lambda idx: meta_t[idx].astype(jnp.int32)
    e1, e2, r1, r2 = field(META_E1), field(META_E2), field(META_R1), field(META_R2)
    counts = cnt[0, :N_EXPERTS].astype(jnp.int32)
    tiles_e = (counts + tm - 1) // tm
    padded = tiles_e * tm
    starts = jnp.cumsum(padded) - padded
    tile_end = jnp.cumsum(tiles_e)
    n_used = tile_end[-1]
    n_tiles = (TOPK_SLOTS * T) // tm + N_EXPERTS
    tid = jnp.minimum(jnp.arange(n_tiles, dtype=jnp.int32), n_used - 1)
    tile_expert = jnp.sum((tid[:, None] >= tile_end[None, :]).astype(jnp.int32), axis=1)
    before = base[:, 0, :N_EXPERTS].astype(jnp.int32)
    run_n = jnp.concatenate([before[1:], counts[None, :]], axis=0) - before
    run_off = jnp.cumsum(run_n, axis=1) - run_n
    runs = _Runs(slot8=((starts[None, :] + before) * SEGS).reshape(-1),
                 n=run_n.reshape(-1), off8=(run_off * SEGS).reshape(-1))
    to_stage = run_off - before

    def stage_row8(e, r):
        e = e.reshape(-1, ts)
        sel = sum(jnp.where(e == k, to_stage[:, k:k + 1], 0) for k in range(N_EXPERTS))
        return ((sel + r.reshape(-1, ts)) * SEGS).reshape(-1)

    sidx1_8 = stage_row8(e1, r1)
    sidx2_8 = stage_row8(e2, r2)
    pad0_8 = (starts + counts) * SEGS
    padn = padded - counts
    tail = jnp.stack([n_used * (tm * SEGS), (n_tiles - n_used) * (tm // DISP_ZB)])
    xs = _dispatch(x2, runs, sidx1_8, sidx2_8, pad0_8, padn, tail, n_tiles * tm)
    ys = _moe(xs, tile_expert, n_used.reshape(1), wg, wu, wd, tm)
    return _combine(ys, runs, sidx1_8, sidx2_8, meta_t[META_G1], meta_t[META_G2], x2, g, b)


TOPK_SLOTS = 2


def kernel(x, attn_w_qkv, attn_sink, attn_w_o, conv_pw1_w, conv_pw1_b, conv_dw_w, conv_dw_b,
           conv_norm_g, conv_norm_b, conv_pw2_w, conv_pw2_b, ffn_w_gate, ffn_w_up, ffn_w_down,
           moe_router, moe_w_gate, moe_w_up, moe_w_down, ln_mix_g, ln_mix_b, ln_ffn_g, ln_ffn_b):
    batch, seq, _ = x.shape
    x2 = x.reshape(batch * seq, D_MODEL)

    q, k, v = _qkv_rope(x2, attn_w_qkv[0].astype(BF16), seq)
    att = _attention(q, k, v, attn_sink[0], batch, seq)
    x2 = _attn_out_ffn(att, x2, attn_w_o[0].astype(BF16), ln_mix_g[0], ln_mix_b[0],
                       ffn_w_gate[0], ffn_w_up[0], ffn_w_down[0], ln_ffn_g[0], ln_ffn_b[0])

    h = _pw1_glu(x2, conv_pw1_w[0].astype(BF16), conv_pw1_b[0])
    x2 = _conv_block(h, x2, conv_dw_w[0], conv_dw_b[0], conv_norm_g[0], conv_norm_b[0],
                     conv_pw2_w[0].astype(BF16), conv_pw2_b[0], ln_mix_g[1], ln_mix_b[1],
                     batch, seq)
    x2 = _moe_block(x2, moe_router[0], moe_w_gate[0], moe_w_up[0], moe_w_down[0],
                    ln_ffn_g[1], ln_ffn_b[1])
    return x2.reshape(batch, seq, D_MODEL)
```

```python
from typing import NamedTuple

import jax
import jax.numpy as jnp
from jax import lax
from jax.experimental import pallas as pl
from jax.experimental.pallas import tpu as pltpu

F32 = jnp.float32
BF16 = jnp.bfloat16

D_MODEL = 1024
N_HEADS = 16
N_KV_HEADS = 4
HEAD_DIM = D_MODEL // N_HEADS
GROUP = N_HEADS // N_KV_HEADS
Q_DIM = N_HEADS * HEAD_DIM
KV_DIM = N_KV_HEADS * HEAD_DIM
WINDOW = 128
BLOCK = 128
ROPE_THETA = 10000.0
CONV_WIDTH = 31
CONV_HALF = CONV_WIDTH // 2
D_FF = 3584
N_EXPERTS = 8
LN_EPS = 1e-5
DEPTH = 2
DN_ALPHA = (2 * DEPTH) ** 0.25
NEG_INF = -1e30

LOG2E = 1.4426950408889634
LANES = 128
SUBLANES = 8
SEGS = D_MODEL // LANES
VAUG_DIM = N_KV_HEADS * LANES
VMEM_LIMIT = 56 * 1024 * 1024


def _params(*sem):
    return pltpu.CompilerParams(dimension_semantics=sem, vmem_limit_bytes=VMEM_LIMIT)


def _layer_norm(y, g, b):
    mu = jnp.mean(y, axis=-1, keepdims=True)
    d = y - mu
    var = jnp.mean(d * d, axis=-1, keepdims=True)
    return d * lax.rsqrt(var + LN_EPS) * g + b


def _silu(v):
    return v * jax.nn.sigmoid(v)


def _qkv_kernel(x_ref, w_ref, cos_ref, sa_ref, sb_ref, q_ref, k_ref, v_ref):
    xb = x_ref[...].astype(BF16)
    acc = jnp.dot(xb, w_ref[...], preferred_element_type=F32)
    cos = cos_ref[...]
    sa = sa_ref[...]
    sb = sb_ref[...]

    def rope(t):
        return (t * cos + pltpu.roll(t, LANES - HEAD_DIM // 2, 1) * sa
                + pltpu.roll(t, HEAD_DIM // 2, 1) * sb)

    scale = HEAD_DIM ** -0.5 * LOG2E
    for c in range(Q_DIM // LANES):
        t = acc[:, c * LANES:(c + 1) * LANES]
        q_ref[:, c * LANES:(c + 1) * LANES] = (rope(t) * scale).astype(BF16)
    for c in range(KV_DIM // LANES):
        t = acc[:, Q_DIM + c * LANES:Q_DIM + (c + 1) * LANES]
        k_ref[:, c * LANES:(c + 1) * LANES] = rope(t).astype(BF16)
    ones = jnp.ones((acc.shape[0], HEAD_DIM), F32)
    for g in range(N_KV_HEADS):
        vg = acc[:, Q_DIM + KV_DIM + g * HEAD_DIM:Q_DIM + KV_DIM + (g + 1) * HEAD_DIM]
        v_ref[:, g * LANES:(g + 1) * LANES] = jnp.concatenate([vg, ones], axis=1).astype(BF16)


def _qkv_rope(x2, w_qkv, seq, tm=1024):
    T = x2.shape[0]
    pos = jnp.arange(seq, dtype=F32)
    inv_freq = ROPE_THETA ** (-jnp.arange(0, HEAD_DIM, 2, dtype=F32) / HEAD_DIM)
    ang = pos[:, None] * inv_freq[None, :]
    cos_h, sin_h = jnp.cos(ang), jnp.sin(ang)
    zero = jnp.zeros_like(sin_h)
    reps = LANES // HEAD_DIM
    cos = jnp.tile(jnp.concatenate([cos_h, cos_h], -1), (1, reps))
    sa = jnp.tile(jnp.concatenate([-sin_h, zero], -1), (1, reps))
    sb = jnp.tile(jnp.concatenate([zero, sin_h], -1), (1, reps))
    nseq = seq // tm
    tab = pl.BlockSpec((tm, LANES), lambda i: (i % nseq, 0))
    n_out = Q_DIM + 2 * KV_DIM
    return pl.pallas_call(
        _qkv_kernel,
        grid=(T // tm,),
        in_specs=[pl.BlockSpec((tm, D_MODEL), lambda i: (i, 0)),
                  pl.BlockSpec((D_MODEL, n_out), lambda i: (0, 0)),
                  tab, tab, tab],
        out_specs=[pl.BlockSpec((tm, Q_DIM), lambda i: (i, 0)),
                   pl.BlockSpec((tm, KV_DIM), lambda i: (i, 0)),
                   pl.BlockSpec((tm, VAUG_DIM), lambda i: (i, 0))],
        out_shape=[jax.ShapeDtypeStruct((T, Q_DIM), BF16),
                   jax.ShapeDtypeStruct((T, KV_DIM), BF16),
                   jax.ShapeDtypeStruct((T, VAUG_DIM), BF16)],
        compiler_params=_params("parallel"),
    )(x2, w_qkv, cos, sa, sb)


ATT_TQ = 1024
ATT_SUB = ATT_TQ // BLOCK
ATT_KEYS = 3 * BLOCK
ATT_ROWS = GROUP * BLOCK


def _attn_kernel(sink_ref, q_ref, kp_ref, km_ref, kn_ref, vp_ref, vm_ref, vn_ref,
                 o_ref, kbuf, vbuf):
    i = pl.program_id(1)
    last = pl.num_programs(1) - 1
    kbuf[0:BLOCK] = kp_ref[...]
    kbuf[BLOCK:BLOCK + ATT_TQ] = km_ref[...]
    kbuf[BLOCK + ATT_TQ:] = kn_ref[...]
    vbuf[0:BLOCK] = vp_ref[...]
    vbuf[BLOCK:BLOCK + ATT_TQ] = vm_ref[...]
    vbuf[BLOCK + ATT_TQ:] = vn_ref[...]

    row = lax.broadcasted_iota(jnp.int32, (ATT_ROWS, BLOCK), 0)
    key = lax.broadcasted_iota(jnp.int32, (ATT_ROWS, BLOCK), 1)
    ql = row & (BLOCK - 1)
    band_prev = key >= ql
    band_next = key <= ql
    hrow = lax.broadcasted_iota(jnp.int32, (ATT_ROWS, 1), 0) // BLOCK

    def block(j, carry):
        r0 = pl.multiple_of(j * BLOCK, BLOCK)
        has_prev = jnp.logical_not((i == 0) & (j == 0))
        has_next = jnp.logical_not((i == last) & (j == ATT_SUB - 1))
        valid_prev = band_prev & has_prev
        valid_next = band_next & has_next
        for g in range(N_KV_HEADS):
            qb = q_ref[pl.ds(r0, BLOCK), g * GROUP * HEAD_DIM:(g + 1) * GROUP * HEAD_DIM]
            q4 = jnp.concatenate(
                [qb[:, h * HEAD_DIM:(h + 1) * HEAD_DIM] for h in range(GROUP)], axis=0)
            kb = kbuf[pl.ds(r0, ATT_KEYS), g * HEAD_DIM:(g + 1) * HEAD_DIM]
            vb = vbuf[pl.ds(r0, ATT_KEYS), g * LANES:(g + 1) * LANES]
            s = lax.dot_general(q4, kb, (((1,), (1,)), ((), ())),
                                preferred_element_type=F32)
            s = jnp.concatenate(
                [jnp.where(valid_prev, s[:, :BLOCK], NEG_INF), s[:, BLOCK:2 * BLOCK],
                 jnp.where(valid_next, s[:, 2 * BLOCK:], NEG_INF)], axis=1)
            sk = jnp.zeros((ATT_ROWS, 1), F32)
            for h in range(GROUP):
                sk = jnp.where(hrow == h, sink_ref[g * GROUP + h] * LOG2E, sk)
            m = jnp.maximum(jnp.max(s, axis=-1, keepdims=True), sk)
            p = jnp.exp2(s - m)
            oa = jnp.dot(p.astype(BF16), vb, preferred_element_type=F32)
            den = oa[:, HEAD_DIM:] + jnp.exp2(sk - m)
            o = oa[:, :HEAD_DIM] / den
            o_ref[pl.ds(r0, BLOCK), g * GROUP * HEAD_DIM:(g + 1) * GROUP * HEAD_DIM] = (
                jnp.concatenate([o[h * BLOCK:(h + 1) * BLOCK] for h in range(GROUP)],
                                axis=1).astype(BF16))
        return carry

    lax.fori_loop(0, ATT_SUB, block, 0)


def _attention(q, k, v, sink, batch, seq):
    nb = seq // BLOCK
    q3 = q.reshape(batch, seq, Q_DIM)
    k3 = k.reshape(batch, seq, KV_DIM)
    v3 = v.reshape(batch, seq, VAUG_DIM)
    prev = lambda w: pl.BlockSpec((None, BLOCK, w),
                                  lambda b, i, s: (b, jnp.maximum(i * ATT_SUB - 1, 0), 0))
    main = lambda w: pl.BlockSpec((None, ATT_TQ, w), lambda b, i, s: (b, i, 0))
    nxt = lambda w: pl.BlockSpec(
        (None, BLOCK, w), lambda b, i, s: (b, jnp.minimum((i + 1) * ATT_SUB, nb - 1), 0))
    out = pl.pallas_call(
        _attn_kernel,
        grid_spec=pltpu.PrefetchScalarGridSpec(
            num_scalar_prefetch=1,
            grid=(batch, seq // ATT_TQ),
            in_specs=[pl.BlockSpec((None, ATT_TQ, Q_DIM), lambda b, i, s: (b, i, 0)),
                      prev(KV_DIM), main(KV_DIM), nxt(KV_DIM),
                      prev(VAUG_DIM), main(VAUG_DIM), nxt(VAUG_DIM)],
            out_specs=pl.BlockSpec((None, ATT_TQ, Q_DIM), lambda b, i, s: (b, i, 0)),
            scratch_shapes=[pltpu.VMEM((ATT_TQ + 2 * BLOCK, KV_DIM), BF16),
                            pltpu.VMEM((ATT_TQ + 2 * BLOCK, VAUG_DIM), BF16)]),
        out_shape=jax.ShapeDtypeStruct((batch, seq, Q_DIM), BF16),
        compiler_params=_params("parallel", "parallel"),
    )(sink.astype(F32), q3, k3, k3, k3, v3, v3, v3)
    return out.reshape(batch * seq, Q_DIM)


FFN_FC = 512
FFN_TM = 1024
FFN_RC = 256


def _swiglu_step(xb, wg_ref, wu_ref, wd_ref):
    gate = jnp.dot(xb, wg_ref[...].astype(BF16), preferred_element_type=F32)
    up = jnp.dot(xb, wu_ref[...].astype(BF16), preferred_element_type=F32)
    h = (_silu(gate) * up).astype(BF16)
    return jnp.dot(h, wd_ref[...].astype(BF16), preferred_element_type=F32)


def _attn_out_ffn_kernel(att_ref, x_ref, wo_ref, g1_ref, b1_ref, wg_ref, wu_ref, wd_ref,
                         g2_ref, b2_ref, o_ref, x1_scr, xb_scr, acc_scr):
    k = pl.program_id(1)

    @pl.when(k == 0)
    def _():
        for r in range(0, att_ref.shape[0], FFN_RC):
            rows = slice(r, r + FFN_RC)
            y = jnp.dot(att_ref[rows, :], wo_ref[...], preferred_element_type=F32)
            x1 = _layer_norm(DN_ALPHA * x_ref[rows, :] + y, g1_ref[...], b1_ref[...])
            x1_scr[rows, :] = x1
            xb_scr[rows, :] = x1.astype(BF16)
        acc_scr[...] = jnp.zeros_like(acc_scr)

    acc_scr[...] += _swiglu_step(xb_scr[...], wg_ref, wu_ref, wd_ref)

    @pl.when(k == pl.num_programs(1) - 1)
    def _():
        y = DN_ALPHA * x1_scr[...] + acc_scr[...]
        o_ref[...] = _layer_norm(y, g2_ref[...], b2_ref[...])


def _attn_out_ffn(att, x2, wo, g1, b1, wg, wu, wd, g2, b2):
    T = x2.shape[0]
    tm, fc = min(FFN_TM, T), FFN_FC
    vec = pl.BlockSpec((1, D_MODEL), lambda i, k: (0, 0))
    row = pl.BlockSpec((tm, D_MODEL), lambda i, k: (i, 0))
    return pl.pallas_call(
        _attn_out_ffn_kernel,
        grid=(T // tm, D_FF // fc),
        in_specs=[row, row,
                  pl.BlockSpec((D_MODEL, D_MODEL), lambda i, k: (0, 0)),
                  vec, vec,
                  pl.BlockSpec((D_MODEL, fc), lambda i, k: (0, k)),
                  pl.BlockSpec((D_MODEL, fc), lambda i, k: (0, k)),
                  pl.BlockSpec((fc, D_MODEL), lambda i, k: (k, 0)),
                  vec, vec],
        out_specs=row,
        out_shape=jax.ShapeDtypeStruct((T, D_MODEL), F32),
        scratch_shapes=[pltpu.VMEM((tm, D_MODEL), F32), pltpu.VMEM((tm, D_MODEL), BF16),
                        pltpu.VMEM((tm, D_MODEL), F32)],
        compiler_params=_params("parallel", "arbitrary"),
    )(att, x2, wo, g1.reshape(1, -1), b1.reshape(1, -1), wg, wu, wd,
      g2.reshape(1, -1), b2.reshape(1, -1))


def _pw1_glu_kernel(x_ref, w_ref, bias_ref, o_ref):
    h = jnp.dot(x_ref[...].astype(BF16), w_ref[...], preferred_element_type=F32) + bias_ref[...]
    o_ref[...] = h[:, :D_MODEL] * jax.nn.sigmoid(h[:, D_MODEL:])


def _pw1_glu(x2, w, bias, tm=1024):
    T = x2.shape[0]
    return pl.pallas_call(
        _pw1_glu_kernel,
        grid=(T // tm,),
        in_specs=[pl.BlockSpec((tm, D_MODEL), lambda i: (i, 0)),
                  pl.BlockSpec((D_MODEL, 2 * D_MODEL), lambda i: (0, 0)),
                  pl.BlockSpec((1, 2 * D_MODEL), lambda i: (0, 0))],
        out_specs=pl.BlockSpec((tm, D_MODEL), lambda i: (i, 0)),
        out_shape=jax.ShapeDtypeStruct((T, D_MODEL), F32),
        compiler_params=_params("parallel"),
    )(x2, w, bias.reshape(1, -1))


CONV_TM = 512
CONV_HALO = 16
CONV_RC = 64
CONV_PARTS = 2
CONV_SHIFT_ROWS = CONV_TM + 2 * CONV_HALO - SUBLANES


def _conv_kernel(hp_ref, hm_ref, hn_ref, x_ref, dw_ref, dwb_ref, ng_ref, nb_ref,
                 w2_ref, b2_ref, g_ref, b_ref, o_ref, hbuf, hshift, cbuf):
    i = pl.program_id(1)
    last = pl.num_programs(1) - 1
    tm = CONV_TM
    hbuf[0:CONV_HALO] = jnp.where(i > 0, hp_ref[...], 0.0)
    hbuf[CONV_HALO:CONV_HALO + tm] = hm_ref[...]
    hbuf[CONV_HALO + tm:] = jnp.where(i < last, hn_ref[...], 0.0)
    for sh in range(1, SUBLANES):
        hshift[sh - 1] = hbuf[sh:sh + CONV_SHIFT_ROWS, :]
    off = CONV_HALO - CONV_HALF

    def rows(r, carry):
        r0 = pl.multiple_of(r * CONV_RC, CONV_RC)
        for c in range(SEGS):
            cols = slice(c * LANES, (c + 1) * LANES)
            parts = [None] * CONV_PARTS
            for sh in range(SUBLANES):
                taps = [w for w in range(CONV_WIDTH) if (off + w) % SUBLANES == sh]
                src = hbuf if sh == 0 else hshift.at[sh - 1]
                span = (off + taps[-1]) // SUBLANES * SUBLANES + CONV_RC
                seg = src[pl.ds(r0, span), cols]
                for w in taps:
                    lo = (off + w) // SUBLANES * SUBLANES
                    term = seg[lo:lo + CONV_RC] * dw_ref[w:w + 1, cols]
                    k = w % CONV_PARTS
                    parts[k] = term if parts[k] is None else parts[k] + term
            acc = parts[0]
            for part in parts[1:]:
                acc = acc + part
            cbuf[pl.ds(r0, CONV_RC), cols] = acc + dwb_ref[:, cols]
        return carry

    lax.fori_loop(0, tm // CONV_RC, rows, 0)
    h = _silu(_layer_norm(cbuf[...], ng_ref[...], nb_ref[...]))
    y = jnp.dot(h.astype(BF16), w2_ref[...], preferred_element_type=F32) + b2_ref[...]
    y = DN_ALPHA * x_ref[...] + y
    o_ref[...] = _layer_norm(y, g_ref[...], b_ref[...])


def _conv_block(h, x2, dw_w, dw_b, ng, nb, w2, b2, g, b, batch, seq):
    tm = CONV_TM
    h3 = h.reshape(batch, seq, D_MODEL)
    x3 = x2.reshape(batch, seq, D_MODEL)
    per = tm // CONV_HALO
    nh = seq // CONV_HALO
    vec = pl.BlockSpec((1, D_MODEL), lambda bb, i: (0, 0))
    out = pl.pallas_call(
        _conv_kernel,
        grid=(batch, seq // tm),
        in_specs=[pl.BlockSpec((None, CONV_HALO, D_MODEL),
                               lambda bb, i: (bb, jnp.maximum(i * per - 1, 0), 0)),
                  pl.BlockSpec((None, tm, D_MODEL), lambda bb, i: (bb, i, 0)),
                  pl.BlockSpec((None, CONV_HALO, D_MODEL),
                               lambda bb, i: (bb, jnp.minimum((i + 1) * per, nh - 1), 0)),
                  pl.BlockSpec((None, tm, D_MODEL), lambda bb, i: (bb, i, 0)),
                  pl.BlockSpec((CONV_WIDTH, D_MODEL), lambda bb, i: (0, 0)),
                  vec, vec, vec,
                  pl.BlockSpec((D_MODEL, D_MODEL), lambda bb, i: (0, 0)),
                  vec, vec, vec],
        out_specs=pl.BlockSpec((None, tm, D_MODEL), lambda bb, i: (bb, i, 0)),
        out_shape=jax.ShapeDtypeStruct((batch, seq, D_MODEL), F32),
        scratch_shapes=[pltpu.VMEM((tm + 2 * CONV_HALO, D_MODEL), F32),
                        pltpu.VMEM((SUBLANES - 1, CONV_SHIFT_ROWS, D_MODEL), F32),
                        pltpu.VMEM((tm, D_MODEL), F32)],
        compiler_params=_params("parallel", "parallel"),
    )(h3, h3, h3, x3, dw_w, dw_b.reshape(1, -1), ng.reshape(1, -1), nb.reshape(1, -1),
      w2, b2.reshape(1, -1), g.reshape(1, -1), b.reshape(1, -1))
    return out.reshape(batch * seq, D_MODEL)


ROUTE_TS = 512
ROUTE_BITS = ROUTE_TS.bit_length()
META_E1, META_E2, META_R1, META_R2, META_G1, META_G2 = range(6)


def _router_kernel(x_ref, w_ref, meta_t_ref, base_ref, cnt_ref, base_scr):
    t = pl.program_id(0)

    @pl.when(t == 0)
    def _():
        base_scr[...] = jnp.zeros_like(base_scr)

    base_ref[...] = base_scr[...]

    tm = x_ref.shape[0]
    x = x_ref[...]
    w = w_ref[...]
    xh = x.astype(BF16)
    xl = (x - xh.astype(F32)).astype(BF16)
    wh = w.astype(BF16)
    wl = (w - wh.astype(F32)).astype(BF16)
    logits = (jnp.dot(xh, wh, preferred_element_type=F32)
              + (jnp.dot(xh, wl, preferred_element_type=F32)
                 + jnp.dot(xl, wh, preferred_element_type=F32)))
    lane = lax.broadcasted_iota(jnp.int32, (tm, LANES), 1)
    lg = jnp.where(lane < N_EXPERTS, logits, -jnp.inf)
    m1 = jnp.max(lg, axis=-1, keepdims=True)
    i1 = jnp.min(jnp.where(lg == m1, lane, LANES), axis=-1, keepdims=True)
    oh1 = lane == i1
    lg2 = jnp.where(oh1, -jnp.inf, lg)
    m2 = jnp.max(lg2, axis=-1, keepdims=True)
    i2 = jnp.min(jnp.where(lg2 == m2, lane, LANES), axis=-1, keepdims=True)
    oh2 = lane == i2
    e = jnp.exp(m2 - m1)
    g1 = 1.0 / (1.0 + e)
    g2 = e / (1.0 + e)
    oh = jnp.where(oh1 | oh2, 1.0, 0.0)
    r = lax.broadcasted_iota(jnp.int32, (tm, tm), 0)
    c = lax.broadcasted_iota(jnp.int32, (tm, tm), 1)
    lower = jnp.where(c < r, 1.0, 0.0).astype(BF16)
    before = jnp.dot(lower, oh.astype(BF16), preferred_element_type=F32) + base_scr[...]
    rank1 = jnp.sum(jnp.where(oh1, before, 0.0), axis=-1, keepdims=True)
    rank2 = jnp.sum(jnp.where(oh2, before, 0.0), axis=-1, keepdims=True)
    base_scr[...] += jnp.sum(oh, axis=0, keepdims=True)
    meta = jnp.zeros((tm, LANES), F32)
    for idx, val in ((META_E1, i1.astype(F32)), (META_E2, i2.astype(F32)),
                     (META_R1, rank1), (META_R2, rank2), (META_G1, g1), (META_G2, g2)):
        meta = jnp.where(lane == idx, val, meta)
    meta_t_ref[...] = jnp.transpose(meta)[:SUBLANES]
    cnt_ref[...] = base_scr[...]


def _router(x2, w_router):
    T = x2.shape[0]
    tm = min(ROUTE_TS, T)
    w = jnp.zeros((D_MODEL, LANES), F32).at[:, :N_EXPERTS].set(w_router.astype(F32))
    return pl.pallas_call(
        _router_kernel,
        grid=(T // tm,),
        in_specs=[pl.BlockSpec((tm, D_MODEL), lambda i: (i, 0)),
                  pl.BlockSpec((D_MODEL, LANES), lambda i: (0, 0))],
        out_specs=[pl.BlockSpec((SUBLANES, tm), lambda i: (0, i)),
                   pl.BlockSpec((None, 1, LANES), lambda i: (i, 0, 0)),
                   pl.BlockSpec((1, LANES), lambda i: (0, 0))],
        out_shape=[jax.ShapeDtypeStruct((SUBLANES, T), F32),
                   jax.ShapeDtypeStruct((T // tm, 1, LANES), F32),
                   jax.ShapeDtypeStruct((1, LANES), F32)],
        scratch_shapes=[pltpu.VMEM((1, LANES), F32)],
        compiler_params=_params("arbitrary"),
    )(x2, w)


DISP_ZB = 128


class _Runs(NamedTuple):
    slot8: jax.Array
    n: jax.Array
    off8: jax.Array


def _row_copy(src, src_row8, dst, dst_row8, sem):
    return pltpu.make_async_copy(
        src.at[pl.ds(pl.multiple_of(src_row8, SUBLANES), SEGS), :],
        dst.at[pl.ds(pl.multiple_of(dst_row8, SUBLANES), SEGS), :], sem)


def _run_copies(n, src, src0_8, dst, dst0_8, sem, priority):
    for bit in range(ROUTE_BITS):
        size = 1 << bit

        @pl.when((n & size) != 0)
        def _():
            above8 = ((n >> (bit + 1)) << (bit + 1)) * SEGS
            pltpu.make_async_copy(
                src.at[pl.ds(pl.multiple_of(src0_8 + above8, SUBLANES), size * SEGS), :],
                dst.at[pl.ds(pl.multiple_of(dst0_8 + above8, SUBLANES), size * SEGS), :],
                sem).start(priority=priority)


ROUTE_UNROLL = 8


def _dispatch_kernel(pad0_ref, padn_ref, tail_ref, run_dst_ref, run_n_ref, run_off_ref,
                     s1_ref, s2_ref, x_ref, xs_ref, sbuf, stage, zbuf, sem, zsem):
    i = pl.program_id(0)
    ts = x_ref.shape[0]
    slot = i % 2
    runs = stage.at[slot]
    for s in range(SEGS):
        sbuf[pl.ds(s, ts, stride=SEGS), :] = x_ref[:, s * LANES:(s + 1) * LANES]

    def place(c, carry):
        for u in range(ROUTE_UNROLL):
            r = c * ROUTE_UNROLL + u
            tile = sbuf[pl.ds(pl.multiple_of(r * SEGS, SUBLANES), SEGS), :]
            runs[pl.ds(pl.multiple_of(s1_ref[r], SUBLANES), SEGS), :] = tile
            runs[pl.ds(pl.multiple_of(s2_ref[r], SUBLANES), SEGS), :] = tile
        return carry

    lax.fori_loop(0, ts // ROUTE_UNROLL, place, 0)
    for e in range(N_EXPERTS):
        idx = i * N_EXPERTS + e
        _run_copies(run_n_ref[idx], runs, run_off_ref[idx], xs_ref, run_dst_ref[idx],
                    sem.at[slot], e % 2)

    def wait_step(which):
        pltpu.make_async_copy(stage.at[which], xs_ref.at[pl.ds(0, 2 * ts * SEGS), :],
                              sem.at[which]).wait()

    @pl.when(pl.program_id(0) == 0)
    def _():
        zbuf[...] = jnp.zeros_like(zbuf)
        for e in range(N_EXPERTS):
            def zstart(j, carry, e=e):
                _row_copy(zbuf, 0, xs_ref, pad0_ref[e] + j * SEGS, zsem).start()
                return carry

            def zwait(j, carry, e=e):
                _row_copy(zbuf, 0, xs_ref, pad0_ref[e] + j * SEGS, zsem).wait()
                return carry

            lax.fori_loop(0, padn_ref[e], zstart, 0)
            lax.fori_loop(0, padn_ref[e], zwait, 0)

        def tail_copy(j):
            rows = DISP_ZB * SEGS
            dst0 = pl.multiple_of(tail_ref[0] + j * rows, SUBLANES)
            return pltpu.make_async_copy(zbuf, xs_ref.at[pl.ds(dst0, rows), :], zsem)

        def tstart(j, carry):
            tail_copy(j).start()
            return carry

        def twait(j, carry):
            tail_copy(j).wait()
            return carry

        lax.fori_loop(0, tail_ref[1], tstart, 0)
        lax.fori_loop(0, tail_ref[1], twait, 0)

    @pl.when(i > 0)
    def _():
        wait_step(1 - slot)

    @pl.when(i == pl.num_programs(0) - 1)
    def _():
        wait_step(slot)


def _dispatch(x2, runs, sidx1_8, sidx2_8, pad0_8, padn, tail, n_slots):
    T = x2.shape[0]
    ts = min(ROUTE_TS, T)
    smem = lambda: pl.BlockSpec((ts,), lambda i, *_: (i,), memory_space=pltpu.SMEM)
    return pl.pallas_call(
        _dispatch_kernel,
        grid_spec=pltpu.PrefetchScalarGridSpec(
            num_scalar_prefetch=6,
            grid=(T // ts,),
            in_specs=[smem(), smem(),
                      pl.BlockSpec((ts, D_MODEL), lambda i, *_: (i, 0))],
            out_specs=pl.BlockSpec(memory_space=pl.ANY),
            scratch_shapes=[pltpu.VMEM((ts * SEGS, LANES), F32),
                            pltpu.VMEM((2, 2 * ts * SEGS, LANES), F32),
                            pltpu.VMEM((DISP_ZB * SEGS, LANES), F32),
                            pltpu.SemaphoreType.DMA((2,)),
                            pltpu.SemaphoreType.DMA(())]),
        out_shape=jax.ShapeDtypeStruct((n_slots * SEGS, LANES), F32),
        compiler_params=_params("arbitrary"),
    )(pad0_8, padn, tail, runs.slot8, runs.n, runs.off8, sidx1_8, sidx2_8, x2)


MOE_TM = 1024
MOE_FC = 512


def _moe_kernel(te_ref, nu_ref, xs_ref, wg_ref, wu_ref, wd_ref, o_ref, xb_scr, acc_scr):
    i = pl.program_id(0)
    k = pl.program_id(1)
    tm = xb_scr.shape[0]

    @pl.when(i < nu_ref[0])
    def _():
        @pl.when(k == 0)
        def _():
            for s in range(SEGS):
                xb_scr[:, s * LANES:(s + 1) * LANES] = (
                    xs_ref[pl.ds(s, tm, stride=SEGS), :].astype(BF16))
            acc_scr[...] = jnp.zeros_like(acc_scr)

        acc_scr[...] += _swiglu_step(xb_scr[...], wg_ref, wu_ref, wd_ref)

        @pl.when(k == pl.num_programs(1) - 1)
        def _():
            for s in range(SEGS):
                o_ref[pl.ds(s, tm, stride=SEGS), :] = acc_scr[:, s * LANES:(s + 1) * LANES]

    @pl.when((i >= nu_ref[0]) & (k == 0))
    def _():
        o_ref[...] = jnp.zeros_like(o_ref)


def _moe(xs, tile_expert, n_used, wg, wu, wd, tm):
    n_tiles = xs.shape[0] // (tm * SEGS)
    fc = MOE_FC
    nk = D_FF // fc

    def tile(i, k, te, nu):
        return (jnp.minimum(i, nu[0] - 1), 0)

    def kk(i, k, nu):
        return jnp.where(i < nu[0], k, nk - 1)

    return pl.pallas_call(
        _moe_kernel,
        grid_spec=pltpu.PrefetchScalarGridSpec(
            num_scalar_prefetch=2,
            grid=(n_tiles, nk),
            in_specs=[pl.BlockSpec((tm * SEGS, LANES), tile),
                      pl.BlockSpec((None, D_MODEL, fc),
                                   lambda i, k, te, nu: (te[i], 0, kk(i, k, nu))),
                      pl.BlockSpec((None, D_MODEL, fc),
                                   lambda i, k, te, nu: (te[i], 0, kk(i, k, nu))),
                      pl.BlockSpec((None, fc, D_MODEL),
                                   lambda i, k, te, nu: (te[i], kk(i, k, nu), 0))],
            out_specs=pl.BlockSpec((tm * SEGS, LANES), lambda i, k, te, nu: (i, 0)),
            scratch_shapes=[pltpu.VMEM((tm, D_MODEL), BF16), pltpu.VMEM((tm, D_MODEL), F32)]),
        out_shape=jax.ShapeDtypeStruct(xs.shape, F32),
        compiler_params=_params("arbitrary", "arbitrary"),
    )(tile_expert, n_used, xs, wg, wu, wd)


def _combine_kernel(run_src_ref, run_n_ref, run_off_ref, s1_ref, s2_ref, g1_ref, g2_ref,
                    ys_ref, x_ref, g_ref, b_ref, o_ref, stage, cbuf, sem):
    i = pl.program_id(0)
    ts = x_ref.shape[0]
    slot = i % 2

    def fetch(step, which):
        for e in range(N_EXPERTS):
            idx = step * N_EXPERTS + e
            _run_copies(run_n_ref[idx], ys_ref, run_src_ref[idx], stage.at[which],
                        run_off_ref[idx], sem.at[which], e % 2)

    @pl.when(i == 0)
    def _():
        fetch(i, slot)

    @pl.when(i < pl.num_programs(0) - 1)
    def _():
        fetch(i + 1, 1 - slot)

    pltpu.make_async_copy(ys_ref.at[pl.ds(0, 2 * ts * SEGS), :], stage.at[slot],
                          sem.at[slot]).wait()
    runs = stage.at[slot]

    def mix(c, carry):
        for u in range(ROUTE_UNROLL):
            r = c * ROUTE_UNROLL + u
            a = runs[pl.ds(pl.multiple_of(s1_ref[r], SUBLANES), SEGS), :]
            bb = runs[pl.ds(pl.multiple_of(s2_ref[r], SUBLANES), SEGS), :]
            cbuf[pl.ds(pl.multiple_of(r * SEGS, SUBLANES), SEGS), :] = (
                g1_ref[r] * a + g2_ref[r] * bb)
        return carry

    lax.fori_loop(0, ts // ROUTE_UNROLL, mix, 0)
    for s in range(SEGS):
        cols = slice(s * LANES, (s + 1) * LANES)
        o_ref[:, cols] = DN_ALPHA * x_ref[:, cols] + cbuf[pl.ds(s, ts, stride=SEGS), :]
    o_ref[...] = _layer_norm(o_ref[...], g_ref[...], b_ref[...])


def _combine(ys, runs, sidx1_8, sidx2_8, gate1, gate2, x2, g, b):
    T = x2.shape[0]
    ts = min(ROUTE_TS, T)
    smem = lambda: pl.BlockSpec((ts,), lambda i, *_: (i,), memory_space=pltpu.SMEM)
    vec = pl.BlockSpec((1, D_MODEL), lambda i, *_: (0, 0))
    return pl.pallas_call(
        _combine_kernel,
        grid_spec=pltpu.PrefetchScalarGridSpec(
            num_scalar_prefetch=3,
            grid=(T // ts,),
            in_specs=[smem(), smem(), smem(), smem(),
                      pl.BlockSpec(memory_space=pl.ANY),
                      pl.BlockSpec((ts, D_MODEL), lambda i, *_: (i, 0)),
                      vec, vec],
            out_specs=pl.BlockSpec((ts, D_MODEL), lambda i, *_: (i, 0)),
            scratch_shapes=[pltpu.VMEM((2, 2 * ts * SEGS, LANES), F32),
                            pltpu.VMEM((ts * SEGS, LANES), F32),
                            pltpu.SemaphoreType.DMA((2,))]),
        out_shape=jax.ShapeDtypeStruct((T, D_MODEL), F32),
        compiler_params=_params("arbitrary"),
    )(runs.slot8, runs.n, runs.off8, sidx1_8, sidx2_8, gate1, gate2, ys, x2,
      g.reshape(1, -1), b.reshape(1, -1))


def _moe_block(x2, w_router, wg, wu, wd, g, b):
    T = x2.shape[0]
    tm = min(MOE_TM, T)
    ts = min(ROUTE_TS, T)
    meta_t, base, cnt = _router(x2, w_router)
    field = lambda idx: meta_t[idx].astype(jnp.int32)
    e1, e2, r1, r2 = field(META_E1), field(META_E2), field(META_R1), field(META_R2)
    counts = cnt[0, :N_EXPERTS].astype(jnp.int32)
    tiles_e = (counts + tm - 1) // tm
    padded = tiles_e * tm
    starts = jnp.cumsum(padded) - padded
    tile_end = jnp.cumsum(tiles_e)
    n_used = tile_end[-1]
    n_tiles = (TOPK_SLOTS * T) // tm + N_EXPERTS
    tid = jnp.minimum(jnp.arange(n_tiles, dtype=jnp.int32), n_used - 1)
    tile_expert = jnp.sum((tid[:, None] >= tile_end[None, :]).astype(jnp.int32), axis=1)
    before = base[:, 0, :N_EXPERTS].astype(jnp.int32)
    run_n = jnp.concatenate([before[1:], counts[None, :]], axis=0) - before
    run_off = jnp.cumsum(run_n, axis=1) - run_n
    runs = _Runs(slot8=((starts[None, :] + before) * SEGS).reshape(-1),
                 n=run_n.reshape(-1), off8=(run_off * SEGS).reshape(-1))
    to_stage = run_off - before

    def stage_row8(e, r):
        e = e.reshape(-1, ts)
        sel = sum(jnp.where(e == k, to_stage[:, k:k + 1], 0) for k in range(N_EXPERTS))
        return ((sel + r.reshape(-1, ts)) * SEGS).reshape(-1)

    sidx1_8 = stage_row8(e1, r1)
    sidx2_8 = stage_row8(e2, r2)
    pad0_8 = (starts + counts) * SEGS
    padn = padded - counts
    tail = jnp.stack([n_used * (tm * SEGS), (n_tiles - n_used) * (tm // DISP_ZB)])
    xs = _dispatch(x2, runs, sidx1_8, sidx2_8, pad0_8, padn, tail, n_tiles * tm)
    ys = _moe(xs, tile_expert, n_used.reshape(1), wg, wu, wd, tm)
    return _combine(ys, runs, sidx1_8, sidx2_8, meta_t[META_G1], meta_t[META_G2], x2, g, b)


TOPK_SLOTS = 2


def kernel(x, attn_w_qkv, attn_sink, attn_w_o, conv_pw1_w, conv_pw1_b, conv_dw_w, conv_dw_b,
           conv_norm_g, conv_norm_b, conv_pw2_w, conv_pw2_b, ffn_w_gate, ffn_w_up, ffn_w_down,
           moe_router, moe_w_gate, moe_w_up, moe_w_down, ln_mix_g, ln_mix_b, ln_ffn_g, ln_ffn_b):
    batch, seq, _ = x.shape
    x2 = x.reshape(batch * seq, D_MODEL)

    q, k, v = _qkv_rope(x2, attn_w_qkv[0].astype(BF16), seq)
    att = _attention(q, k, v, attn_sink[0], batch, seq)
    x2 = _attn_out_ffn(att, x2, attn_w_o[0].astype(BF16), ln_mix_g[0], ln_mix_b[0],
                       ffn_w_gate[0], ffn_w_up[0], ffn_w_down[0], ln_ffn_g[0], ln_ffn_b[0])

    h = _pw1_glu(x2, conv_pw1_w[0].astype(BF16), conv_pw1_b[0])
    x2 = _conv_block(h, x2, conv_dw_w[0], conv_dw_b[0], conv_norm_g[0], conv_norm_b[0],
                     conv_pw2_w[0].astype(BF16), conv_pw2_b[0], ln_mix_g[1], ln_mix_b[1],
                     batch, seq)
    x2 = _moe_block(x2, moe_router[0], moe_w_gate[0], moe_w_up[0], moe_w_down[0],
                    ln_ffn_g[1], ln_ffn_b[1])
    return x2.reshape(batch, seq, D_MODEL)
```

```python
from typing import NamedTuple

import jax
import jax.numpy as jnp
from jax import lax
from jax.experimental import pallas as pl
from jax.experimental.pallas import tpu as pltpu

F32 = jnp.float32
BF16 = jnp.bfloat16

D_MODEL = 1024
N_HEADS = 16
N_KV_HEADS = 4
HEAD_DIM = D_MODEL // N_HEADS
GROUP = N_HEADS // N_KV_HEADS
Q_DIM = N_HEADS * HEAD_DIM
KV_DIM = N_KV_HEADS * HEAD_DIM
WINDOW = 128
BLOCK = 128
ROPE_THETA = 10000.0
CONV_WIDTH = 31
CONV_HALF = CONV_WIDTH // 2
D_FF = 3584
N_EXPERTS = 8
LN_EPS = 1e-5
DEPTH = 2
DN_ALPHA = (2 * DEPTH) ** 0.25
NEG_INF = -1e30

LOG2E = 1.4426950408889634
LANES = 128
SUBLANES = 8
SEGS = D_MODEL // LANES
VAUG_DIM = N_KV_HEADS * LANES
VMEM_LIMIT = 56 * 1024 * 1024


def _params(*sem):
    return pltpu.CompilerParams(dimension_semantics=sem, vmem_limit_bytes=VMEM_LIMIT)


def _layer_norm(y, g, b):
    mu = jnp.mean(y, axis=-1, keepdims=True)
    d = y - mu
    var = jnp.mean(d * d, axis=-1, keepdims=True)
    return d * lax.rsqrt(var + LN_EPS) * g + b


def _silu(v):
    return v * jax.nn.sigmoid(v)


def _qkv_kernel(x_ref, w_ref, cos_ref, sa_ref, sb_ref, q_ref, k_ref, v_ref):
    xb = x_ref[...].astype(BF16)
    acc = jnp.dot(xb, w_ref[...], preferred_element_type=F32)
    cos = cos_ref[...]
    sa = sa_ref[...]
    sb = sb_ref[...]

    def rope(t):
        return (t * cos + pltpu.roll(t, LANES - HEAD_DIM // 2, 1) * sa
                + pltpu.roll(t, HEAD_DIM // 2, 1) * sb)

    scale = HEAD_DIM ** -0.5 * LOG2E
    for c in range(Q_DIM // LANES):
        t = acc[:, c * LANES:(c + 1) * LANES]
        q_ref[:, c * LANES:(c + 1) * LANES] = (rope(t) * scale).astype(BF16)
    for c in range(KV_DIM // LANES):
        t = acc[:, Q_DIM + c * LANES:Q_DIM + (c + 1) * LANES]
        k_ref[:, c * LANES:(c + 1) * LANES] = rope(t).astype(BF16)
    ones = jnp.ones((acc.shape[0], HEAD_DIM), F32)
    for g in range(N_KV_HEADS):
        vg = acc[:, Q_DIM + KV_DIM + g * HEAD_DIM:Q_DIM + KV_DIM + (g + 1) * HEAD_DIM]
        v_ref[:, g * LANES:(g + 1) * LANES] = jnp.concatenate([vg, ones], axis=1).astype(BF16)


def _qkv_rope(x2, w_qkv, seq, tm=1024):
    T = x2.shape[0]
    pos = jnp.arange(seq, dtype=F32)
    inv_freq = ROPE_THETA ** (-jnp.arange(0, HEAD_DIM, 2, dtype=F32) / HEAD_DIM)
    ang = pos[:, None] * inv_freq[None, :]
    cos_h, sin_h = jnp.cos(ang), jnp.sin(ang)
    zero = jnp.zeros_like(sin_h)
    reps = LANES // HEAD_DIM
    cos = jnp.tile(jnp.concatenate([cos_h, cos_h], -1), (1, reps))
    sa = jnp.tile(jnp.concatenate([-sin_h, zero], -1), (1, reps))
    sb = jnp.tile(jnp.concatenate([zero, sin_h], -1), (1, reps))
    nseq = seq // tm
    tab = pl.BlockSpec((tm, LANES), lambda i: (i % nseq, 0))
    n_out = Q_DIM + 2 * KV_DIM
    return pl.pallas_call(
        _qkv_kernel,
        grid=(T // tm,),
        in_specs=[pl.BlockSpec((tm, D_MODEL), lambda i: (i, 0)),
                  pl.BlockSpec((D_MODEL, n_out), lambda i: (0, 0)),
                  tab, tab, tab],
        out_specs=[pl.BlockSpec((tm, Q_DIM), lambda i: (i, 0)),
                   pl.BlockSpec((tm, KV_DIM), lambda i: (i, 0)),
                   pl.BlockSpec((tm, VAUG_DIM), lambda i: (i, 0))],
        out_shape=[jax.ShapeDtypeStruct((T, Q_DIM), BF16),
                   jax.ShapeDtypeStruct((T, KV_DIM), BF16),
                   jax.ShapeDtypeStruct((T, VAUG_DIM), BF16)],
        compiler_params=_params("parallel"),
    )(x2, w_qkv, cos, sa, sb)


ATT_TQ = 1024
ATT_SUB = ATT_TQ // BLOCK
ATT_PAIR = 2
ATT_KEYS = 3 * BLOCK
ATT_ROWS = GROUP * BLOCK


def _attn_kernel(sink_ref, q_ref, kp_ref, km_ref, kn_ref, vp_ref, vm_ref, vn_ref,
                 o_ref, kbuf, vbuf):
    i = pl.program_id(1)
    last = pl.num_programs(1) - 1
    kbuf[0:BLOCK] = kp_ref[...]
    kbuf[BLOCK:BLOCK + ATT_TQ] = km_ref[...]
    kbuf[BLOCK + ATT_TQ:] = kn_ref[...]
    vbuf[0:BLOCK] = vp_ref[...]
    vbuf[BLOCK:BLOCK + ATT_TQ] = vm_ref[...]
    vbuf[BLOCK + ATT_TQ:] = vn_ref[...]

    row = lax.broadcasted_iota(jnp.int32, (ATT_ROWS, BLOCK), 0)
    key = lax.broadcasted_iota(jnp.int32, (ATT_ROWS, BLOCK), 1)
    ql = row & (BLOCK - 1)
    band_prev = key >= ql
    band_next = key <= ql
    hrow = lax.broadcasted_iota(jnp.int32, (ATT_ROWS, 1), 0) // BLOCK
    sinks = []
    for g in range(N_KV_HEADS):
        sk = jnp.zeros((ATT_ROWS, 1), F32)
        for h in range(GROUP):
            sk = jnp.where(hrow == h, sink_ref[g * GROUP + h] * LOG2E, sk)
        sinks.append(sk)

    def blocks(jj, carry):
        pairs = []
        for u in range(ATT_PAIR):
            j = jj * ATT_PAIR + u
            r0 = pl.multiple_of(j * BLOCK, BLOCK)
            has_prev = jnp.logical_not((i == 0) & (j == 0))
            has_next = jnp.logical_not((i == last) & (j == ATT_SUB - 1))
            pairs += [(r0, band_prev & has_prev, band_next & has_next, g)
                      for g in range(N_KV_HEADS)]
        scores = []
        for r0, _, _, g in pairs:
            qb = q_ref[pl.ds(r0, BLOCK), g * GROUP * HEAD_DIM:(g + 1) * GROUP * HEAD_DIM]
            q4 = jnp.concatenate(
                [qb[:, h * HEAD_DIM:(h + 1) * HEAD_DIM] for h in range(GROUP)], axis=0)
            kb = kbuf[pl.ds(r0, ATT_KEYS), g * HEAD_DIM:(g + 1) * HEAD_DIM]
            scores.append(lax.dot_general(q4, kb, (((1,), (1,)), ((), ())),
                                          preferred_element_type=F32))
        probs = []
        for s, (_, valid_prev, valid_next, g) in zip(scores, pairs):
            s = jnp.concatenate(
                [jnp.where(valid_prev, s[:, :BLOCK], NEG_INF), s[:, BLOCK:2 * BLOCK],
                 jnp.where(valid_next, s[:, 2 * BLOCK:], NEG_INF)], axis=1)
            m = jnp.maximum(jnp.max(s, axis=-1, keepdims=True), sinks[g])
            probs.append((jnp.exp2(s - m).astype(BF16), jnp.exp2(sinks[g] - m)))
        for (p, p_sink), (r0, _, _, g) in zip(probs, pairs):
            vb = vbuf[pl.ds(r0, ATT_KEYS), g * LANES:(g + 1) * LANES]
            oa = jnp.dot(p, vb, preferred_element_type=F32)
            o = oa[:, :HEAD_DIM] / (oa[:, HEAD_DIM:] + p_sink)
            o_ref[pl.ds(r0, BLOCK), g * GROUP * HEAD_DIM:(g + 1) * GROUP * HEAD_DIM] = (
                jnp.concatenate([o[h * BLOCK:(h + 1) * BLOCK] for h in range(GROUP)],
                                axis=1).astype(BF16))
        return carry

    lax.fori_loop(0, ATT_SUB // ATT_PAIR, blocks, 0)


def _attention(q, k, v, sink, batch, seq):
    nb = seq // BLOCK
    q3 = q.reshape(batch, seq, Q_DIM)
    k3 = k.reshape(batch, seq, KV_DIM)
    v3 = v.reshape(batch, seq, VAUG_DIM)
    prev = lambda w: pl.BlockSpec((None, BLOCK, w),
                                  lambda b, i, s: (b, jnp.maximum(i * ATT_SUB - 1, 0), 0))
    main = lambda w: pl.BlockSpec((None, ATT_TQ, w), lambda b, i, s: (b, i, 0))
    nxt = lambda w: pl.BlockSpec(
        (None, BLOCK, w), lambda b, i, s: (b, jnp.minimum((i + 1) * ATT_SUB, nb - 1), 0))
    out = pl.pallas_call(
        _attn_kernel,
        grid_spec=pltpu.PrefetchScalarGridSpec(
            num_scalar_prefetch=1,
            grid=(batch, seq // ATT_TQ),
            in_specs=[pl.BlockSpec((None, ATT_TQ, Q_DIM), lambda b, i, s: (b, i, 0)),
                      prev(KV_DIM), main(KV_DIM), nxt(KV_DIM),
                      prev(VAUG_DIM), main(VAUG_DIM), nxt(VAUG_DIM)],
            out_specs=pl.BlockSpec((None, ATT_TQ, Q_DIM), lambda b, i, s: (b, i, 0)),
            scratch_shapes=[pltpu.VMEM((ATT_TQ + 2 * BLOCK, KV_DIM), BF16),
                            pltpu.VMEM((ATT_TQ + 2 * BLOCK, VAUG_DIM), BF16)]),
        out_shape=jax.ShapeDtypeStruct((batch, seq, Q_DIM), BF16),
        compiler_params=_params("parallel", "parallel"),
    )(sink.astype(F32), q3, k3, k3, k3, v3, v3, v3)
    return out.reshape(batch * seq, Q_DIM)


FFN_FC = 512
FFN_TM = 1024
FFN_RC = 256


def _swiglu_step(xb, wg_ref, wu_ref, wd_ref):
    gate = jnp.dot(xb, wg_ref[...].astype(BF16), preferred_element_type=F32)
    up = jnp.dot(xb, wu_ref[...].astype(BF16), preferred_element_type=F32)
    h = (_silu(gate) * up).astype(BF16)
    return jnp.dot(h, wd_ref[...].astype(BF16), preferred_element_type=F32)


def _attn_out_ffn_kernel(att_ref, x_ref, wo_ref, g1_ref, b1_ref, wg_ref, wu_ref, wd_ref,
                         g2_ref, b2_ref, o_ref, x1_scr, xb_scr, acc_scr):
    k = pl.program_id(1)

    @pl.when(k == 0)
    def _():
        for r in range(0, att_ref.shape[0], FFN_RC):
            rows = slice(r, r + FFN_RC)
            y = jnp.dot(att_ref[rows, :], wo_ref[...], preferred_element_type=F32)
            x1 = _layer_norm(DN_ALPHA * x_ref[rows, :] + y, g1_ref[...], b1_ref[...])
            x1_scr[rows, :] = x1
            xb_scr[rows, :] = x1.astype(BF16)
        acc_scr[...] = jnp.zeros_like(acc_scr)

    acc_scr[...] += _swiglu_step(xb_scr[...], wg_ref, wu_ref, wd_ref)

    @pl.when(k == pl.num_programs(1) - 1)
    def _():
        y = DN_ALPHA * x1_scr[...] + acc_scr[...]
        o_ref[...] = _layer_norm(y, g2_ref[...], b2_ref[...])


def _attn_out_ffn(att, x2, wo, g1, b1, wg, wu, wd, g2, b2):
    T = x2.shape[0]
    tm, fc = min(FFN_TM, T), FFN_FC
    vec = pl.BlockSpec((1, D_MODEL), lambda i, k: (0, 0))
    row = pl.BlockSpec((tm, D_MODEL), lambda i, k: (i, 0))
    return pl.pallas_call(
        _attn_out_ffn_kernel,
        grid=(T // tm, D_FF // fc),
        in_specs=[row, row,
                  pl.BlockSpec((D_MODEL, D_MODEL), lambda i, k: (0, 0)),
                  vec, vec,
                  pl.BlockSpec((D_MODEL, fc), lambda i, k: (0, k)),
                  pl.BlockSpec((D_MODEL, fc), lambda i, k: (0, k)),
                  pl.BlockSpec((fc, D_MODEL), lambda i, k: (k, 0)),
                  vec, vec],
        out_specs=row,
        out_shape=jax.ShapeDtypeStruct((T, D_MODEL), F32),
        scratch_shapes=[pltpu.VMEM((tm, D_MODEL), F32), pltpu.VMEM((tm, D_MODEL), BF16),
                        pltpu.VMEM((tm, D_MODEL), F32)],
        compiler_params=_params("parallel", "arbitrary"),
    )(att, x2, wo, g1.reshape(1, -1), b1.reshape(1, -1), wg, wu, wd,
      g2.reshape(1, -1), b2.reshape(1, -1))


def _pw1_glu_kernel(x_ref, w_ref, bias_ref, o_ref):
    h = jnp.dot(x_ref[...].astype(BF16), w_ref[...], preferred_element_type=F32) + bias_ref[...]
    o_ref[...] = h[:, :D_MODEL] * jax.nn.sigmoid(h[:, D_MODEL:])


def _pw1_glu(x2, w, bias, tm=1024):
    T = x2.shape[0]
    return pl.pallas_call(
        _pw1_glu_kernel,
        grid=(T // tm,),
        in_specs=[pl.BlockSpec((tm, D_MODEL), lambda i: (i, 0)),
                  pl.BlockSpec((D_MODEL, 2 * D_MODEL), lambda i: (0, 0)),
                  pl.BlockSpec((1, 2 * D_MODEL), lambda i: (0, 0))],
        out_specs=pl.BlockSpec((tm, D_MODEL), lambda i: (i, 0)),
        out_shape=jax.ShapeDtypeStruct((T, D_MODEL), F32),
        compiler_params=_params("parallel"),
    )(x2, w, bias.reshape(1, -1))


CONV_TM = 512
CONV_HALO = 16
CONV_RC = 64
CONV_PARTS = 2
CONV_SHIFT_ROWS = CONV_TM + 2 * CONV_HALO - SUBLANES


def _conv_kernel(hp_ref, hm_ref, hn_ref, x_ref, dw_ref, dwb_ref, ng_ref, nb_ref,
                 w2_ref, b2_ref, g_ref, b_ref, o_ref, hbuf, hshift, cbuf):
    i = pl.program_id(1)
    last = pl.num_programs(1) - 1
    tm = CONV_TM
    hbuf[0:CONV_HALO] = jnp.where(i > 0, hp_ref[...], 0.0)
    hbuf[CONV_HALO:CONV_HALO + tm] = hm_ref[...]
    hbuf[CONV_HALO + tm:] = jnp.where(i < last, hn_ref[...], 0.0)
    for sh in range(1, SUBLANES):
        hshift[sh - 1] = hbuf[sh:sh + CONV_SHIFT_ROWS, :]
    off = CONV_HALO - CONV_HALF

    def rows(r, carry):
        r0 = pl.multiple_of(r * CONV_RC, CONV_RC)
        for c in range(SEGS):
            cols = slice(c * LANES, (c + 1) * LANES)
            parts = [None] * CONV_PARTS
            for sh in range(SUBLANES):
                taps = [w for w in range(CONV_WIDTH) if (off + w) % SUBLANES == sh]
                src = hbuf if sh == 0 else hshift.at[sh - 1]
                span = (off + taps[-1]) // SUBLANES * SUBLANES + CONV_RC
                seg = src[pl.ds(r0, span), cols]
                for w in taps:
                    lo = (off + w) // SUBLANES * SUBLANES
                    term = seg[lo:lo + CONV_RC] * dw_ref[w:w + 1, cols]
                    k = w % CONV_PARTS
                    parts[k] = term if parts[k] is None else parts[k] + term
            acc = parts[0]
            for part in parts[1:]:
                acc = acc + part
            cbuf[pl.ds(r0, CONV_RC), cols] = acc + dwb_ref[:, cols]
        return carry

    lax.fori_loop(0, tm // CONV_RC, rows, 0)
    h = _silu(_layer_norm(cbuf[...], ng_ref[...], nb_ref[...]))
    y = jnp.dot(h.astype(BF16), w2_ref[...], preferred_element_type=F32) + b2_ref[...]
    y = DN_ALPHA * x_ref[...] + y
    o_ref[...] = _layer_norm(y, g_ref[...], b_ref[...])


def _conv_block(h, x2, dw_w, dw_b, ng, nb, w2, b2, g, b, batch, seq):
    tm = CONV_TM
    h3 = h.reshape(batch, seq, D_MODEL)
    x3 = x2.reshape(batch, seq, D_MODEL)
    per = tm // CONV_HALO
    nh = seq // CONV_HALO
    vec = pl.BlockSpec((1, D_MODEL), lambda bb, i: (0, 0))
    out = pl.pallas_call(
        _conv_kernel,
        grid=(batch, seq // tm),
        in_specs=[pl.BlockSpec((None, CONV_HALO, D_MODEL),
                               lambda bb, i: (bb, jnp.maximum(i * per - 1, 0), 0)),
                  pl.BlockSpec((None, tm, D_MODEL), lambda bb, i: (bb, i, 0)),
                  pl.BlockSpec((None, CONV_HALO, D_MODEL),
                               lambda bb, i: (bb, jnp.minimum((i + 1) * per, nh - 1), 0)),
                  pl.BlockSpec((None, tm, D_MODEL), lambda bb, i: (bb, i, 0)),
                  pl.BlockSpec((CONV_WIDTH, D_MODEL), lambda bb, i: (0, 0)),
                  vec, vec, vec,
                  pl.BlockSpec((D_MODEL, D_MODEL), lambda bb, i: (0, 0)),
                  vec, vec, vec],
        out_specs=pl.BlockSpec((None, tm, D_MODEL), lambda bb, i: (bb, i, 0)),
        out_shape=jax.ShapeDtypeStruct((batch, seq, D_MODEL), F32),
        scratch_shapes=[pltpu.VMEM((tm + 2 * CONV_HALO, D_MODEL), F32),
                        pltpu.VMEM((SUBLANES - 1, CONV_SHIFT_ROWS, D_MODEL), F32),
                        pltpu.VMEM((tm, D_MODEL), F32)],
        compiler_params=_params("parallel", "parallel"),
    )(h3, h3, h3, x3, dw_w, dw_b.reshape(1, -1), ng.reshape(1, -1), nb.reshape(1, -1),
      w2, b2.reshape(1, -1), g.reshape(1, -1), b.reshape(1, -1))
    return out.reshape(batch * seq, D_MODEL)


ROUTE_TS = 512
ROUTE_BITS = ROUTE_TS.bit_length()
META_E1, META_E2, META_R1, META_R2, META_G1, META_G2 = range(6)


def _router_kernel(x_ref, w_ref, meta_t_ref, base_ref, cnt_ref, base_scr):
    t = pl.program_id(0)

    @pl.when(t == 0)
    def _():
        base_scr[...] = jnp.zeros_like(base_scr)

    base_ref[...] = base_scr[...]

    tm = x_ref.shape[0]
    x = x_ref[...]
    xh = x.astype(BF16)
    xl = (x - xh.astype(F32)).astype(BF16)
    parts = jnp.dot(jnp.concatenate([xh, xl], axis=0), w_ref[...],
                    preferred_element_type=F32)
    logits = ((parts[:tm, :LANES] + parts[:tm, LANES:])
              + (parts[tm:, :LANES] + parts[tm:, LANES:]))
    lane = lax.broadcasted_iota(jnp.int32, (tm, LANES), 1)
    lane_f = lane.astype(F32)
    lg = jnp.where(lane < N_EXPERTS, logits, -jnp.inf)
    m1 = jnp.max(lg, axis=-1, keepdims=True)
    i1 = jnp.min(jnp.where(lg == m1, lane_f, float(LANES)), axis=-1, keepdims=True)
    oh1 = lane_f == i1
    lg2 = jnp.where(oh1, -jnp.inf, lg)
    m2 = jnp.max(lg2, axis=-1, keepdims=True)
    i2 = jnp.min(jnp.where(lg2 == m2, lane_f, float(LANES)), axis=-1, keepdims=True)
    oh2 = lane_f == i2
    e = jnp.exp(m2 - m1)
    g1 = 1.0 / (1.0 + e)
    g2 = e / (1.0 + e)
    oh = jnp.where(oh1 | oh2, 1.0, 0.0)
    r = lax.broadcasted_iota(jnp.int32, (tm, tm), 0)
    c = lax.broadcasted_iota(jnp.int32, (tm, tm), 1)
    lower = jnp.where(c < r, 1.0, 0.0).astype(BF16)
    before = jnp.dot(lower, oh.astype(BF16), preferred_element_type=F32) + base_scr[...]
    rank1 = jnp.sum(jnp.where(oh1, before, 0.0), axis=-1, keepdims=True)
    rank2 = jnp.sum(jnp.where(oh2, before, 0.0), axis=-1, keepdims=True)
    base_scr[...] += jnp.sum(oh, axis=0, keepdims=True)
    meta = jnp.zeros((tm, LANES), F32)
    for idx, val in ((META_E1, i1), (META_E2, i2),
                     (META_R1, rank1), (META_R2, rank2), (META_G1, g1), (META_G2, g2)):
        meta = jnp.where(lane == idx, val, meta)
    meta_t_ref[...] = jnp.transpose(meta)[:SUBLANES]
    cnt_ref[...] = base_scr[...]


def _router(x2, w_router):
    T = x2.shape[0]
    tm = min(ROUTE_TS, T)
    w = jnp.zeros((D_MODEL, LANES), F32).at[:, :N_EXPERTS].set(w_router.astype(F32))
    w_hi = w.astype(BF16)
    w_lo = (w - w_hi.astype(F32)).astype(BF16)
    w = jnp.concatenate([w_hi, w_lo], axis=1)
    return pl.pallas_call(
        _router_kernel,
        grid=(T // tm,),
        in_specs=[pl.BlockSpec((tm, D_MODEL), lambda i: (i, 0)),
                  pl.BlockSpec((D_MODEL, 2 * LANES), lambda i: (0, 0))],
        out_specs=[pl.BlockSpec((SUBLANES, tm), lambda i: (0, i)),
                   pl.BlockSpec((None, 1, LANES), lambda i: (i, 0, 0)),
                   pl.BlockSpec((1, LANES), lambda i: (0, 0))],
        out_shape=[jax.ShapeDtypeStruct((SUBLANES, T), F32),
                   jax.ShapeDtypeStruct((T // tm, 1, LANES), F32),
                   jax.ShapeDtypeStruct((1, LANES), F32)],
        scratch_shapes=[pltpu.VMEM((1, LANES), F32)],
        compiler_params=_params("arbitrary"),
    )(x2, w)


DISP_ZB = 128


class _Runs(NamedTuple):
    slot8: jax.Array
    n: jax.Array
    off8: jax.Array


def _row_copy(src, src_row8, dst, dst_row8, sem):
    return pltpu.make_async_copy(
        src.at[pl.ds(pl.multiple_of(src_row8, SUBLANES), SEGS), :],
        dst.at[pl.ds(pl.multiple_of(dst_row8, SUBLANES), SEGS), :], sem)


def _run_copies(n, src, src0_8, dst, dst0_8, sem, priority):
    for bit in range(ROUTE_BITS):
        size = 1 << bit

        @pl.when((n & size) != 0)
        def _():
            above8 = ((n >> (bit + 1)) << (bit + 1)) * SEGS
            pltpu.make_async_copy(
                src.at[pl.ds(pl.multiple_of(src0_8 + above8, SUBLANES), size * SEGS), :],
                dst.at[pl.ds(pl.multiple_of(dst0_8 + above8, SUBLANES), size * SEGS), :],
                sem).start(priority=priority)


ROUTE_UNROLL = 8


def _dispatch_kernel(pad0_ref, padn_ref, tail_ref, run_dst_ref, run_n_ref, run_off_ref,
                     s1_ref, s2_ref, x_ref, xs_ref, sbuf, stage, zbuf, sem, zsem):
    i = pl.program_id(0)
    ts = x_ref.shape[0]
    slot = i % 2
    runs = stage.at[slot]
    for s in range(SEGS):
        sbuf[pl.ds(s, ts, stride=SEGS), :] = x_ref[:, s * LANES:(s + 1) * LANES]

    def place(c, carry):
        for u in range(ROUTE_UNROLL):
            r = c * ROUTE_UNROLL + u
            tile = sbuf[pl.ds(pl.multiple_of(r * SEGS, SUBLANES), SEGS), :]
            runs[pl.ds(pl.multiple_of(s1_ref[r], SUBLANES), SEGS), :] = tile
            runs[pl.ds(pl.multiple_of(s2_ref[r], SUBLANES), SEGS), :] = tile
        return carry

    lax.fori_loop(0, ts // ROUTE_UNROLL, place, 0)
    for e in range(N_EXPERTS):
        idx = i * N_EXPERTS + e
        _run_copies(run_n_ref[idx], runs, run_off_ref[idx], xs_ref, run_dst_ref[idx],
                    sem.at[slot], e % 2)

    def wait_step(which):
        pltpu.make_async_copy(stage.at[which], xs_ref.at[pl.ds(0, 2 * ts * SEGS), :],
                              sem.at[which]).wait()

    @pl.when(pl.program_id(0) == 0)
    def _():
        zbuf[...] = jnp.zeros_like(zbuf)
        for e in range(N_EXPERTS):
            def zstart(j, carry, e=e):
                _row_copy(zbuf, 0, xs_ref, pad0_ref[e] + j * SEGS, zsem).start()
                return carry

            def zwait(j, carry, e=e):
                _row_copy(zbuf, 0, xs_ref, pad0_ref[e] + j * SEGS, zsem).wait()
                return carry

            lax.fori_loop(0, padn_ref[e], zstart, 0)
            lax.fori_loop(0, padn_ref[e], zwait, 0)

        def tail_copy(j):
            rows = DISP_ZB * SEGS
            dst0 = pl.multiple_of(tail_ref[0] + j * rows, SUBLANES)
            return pltpu.make_async_copy(zbuf, xs_ref.at[pl.ds(dst0, rows), :], zsem)

        def tstart(j, carry):
            tail_copy(j).start()
            return carry

        def twait(j, carry):
            tail_copy(j).wait()
            return carry

        lax.fori_loop(0, tail_ref[1], tstart, 0)
        lax.fori_loop(0, tail_ref[1], twait, 0)

    @pl.when(i > 0)
    def _():
        wait_step(1 - slot)

    @pl.when(i == pl.num_programs(0) - 1)
    def _():
        wait_step(slot)


def _dispatch(x2, runs, sidx1_8, sidx2_8, pad0_8, padn, tail, n_slots):
    T = x2.shape[0]
    ts = min(ROUTE_TS, T)
    smem = lambda: pl.BlockSpec((ts,), lambda i, *_: (i,), memory_space=pltpu.SMEM)
    return pl.pallas_call(
        _dispatch_kernel,
        grid_spec=pltpu.PrefetchScalarGridSpec(
            num_scalar_prefetch=6,
            grid=(T // ts,),
            in_specs=[smem(), smem(),
                      pl.BlockSpec((ts, D_MODEL), lambda i, *_: (i, 0))],
            out_specs=pl.BlockSpec(memory_space=pl.ANY),
            scratch_shapes=[pltpu.VMEM((ts * SEGS, LANES), F32),
                            pltpu.VMEM((2, 2 * ts * SEGS, LANES), F32),
                            pltpu.VMEM((DISP_ZB * SEGS, LANES), F32),
                            pltpu.SemaphoreType.DMA((2,)),
                            pltpu.SemaphoreType.DMA(())]),
        out_shape=jax.ShapeDtypeStruct((n_slots * SEGS, LANES), F32),
        compiler_params=_params("arbitrary"),
    )(pad0_8, padn, tail, runs.slot8, runs.n, runs.off8, sidx1_8, sidx2_8, x2)


MOE_TM = 1024
MOE_FC = 512


def _moe_kernel(te_ref, nu_ref, xs_ref, wg_ref, wu_ref, wd_ref, o_ref, xb_scr, acc_scr):
    i = pl.program_id(0)
    k = pl.program_id(1)
    tm = xb_scr.shape[0]

    @pl.when(i < nu_ref[0])
    def _():
        @pl.when(k == 0)
        def _():
            for s in range(SEGS):
                xb_scr[:, s * LANES:(s + 1) * LANES] = (
                    xs_ref[pl.ds(s, tm, stride=SEGS), :].astype(BF16))
            acc_scr[...] = jnp.zeros_like(acc_scr)

        acc_scr[...] += _swiglu_step(xb_scr[...], wg_ref, wu_ref, wd_ref)

        @pl.when(k == pl.num_programs(1) - 1)
        def _():
            for s in range(SEGS):
                o_ref[pl.ds(s, tm, stride=SEGS), :] = acc_scr[:, s * LANES:(s + 1) * LANES]

    @pl.when((i >= nu_ref[0]) & (k == 0))
    def _():
        o_ref[...] = jnp.zeros_like(o_ref)


def _moe(xs, tile_expert, n_used, wg, wu, wd, tm):
    n_tiles = xs.shape[0] // (tm * SEGS)
    fc = MOE_FC
    nk = D_FF // fc

    def tile(i, k, te, nu):
        return (jnp.minimum(i, nu[0] - 1), 0)

    def kk(i, k, nu):
        return jnp.where(i < nu[0], k, nk - 1)

    return pl.pallas_call(
        _moe_kernel,
        grid_spec=pltpu.PrefetchScalarGridSpec(
            num_scalar_prefetch=2,
            grid=(n_tiles, nk),
            in_specs=[pl.BlockSpec((tm * SEGS, LANES), tile),
                      pl.BlockSpec((None, D_MODEL, fc),
                                   lambda i, k, te, nu: (te[i], 0, kk(i, k, nu))),
                      pl.BlockSpec((None, D_MODEL, fc),
                                   lambda i, k, te, nu: (te[i], 0, kk(i, k, nu))),
                      pl.BlockSpec((None, fc, D_MODEL),
                                   lambda i, k, te, nu: (te[i], kk(i, k, nu), 0))],
            out_specs=pl.BlockSpec((tm * SEGS, LANES), lambda i, k, te, nu: (i, 0)),
            scratch_shapes=[pltpu.VMEM((tm, D_MODEL), BF16), pltpu.VMEM((tm, D_MODEL), F32)]),
        out_shape=jax.ShapeDtypeStruct(xs.shape, F32),
        compiler_params=_params("arbitrary", "arbitrary"),
    )(tile_expert, n_used, xs, wg, wu, wd)


def _combine_kernel(run_src_ref, run_n_ref, run_off_ref, s1_ref, s2_ref, g1_ref, g2_ref,
                    ys_ref, x_ref, g_ref, b_ref, o_ref, stage, cbuf, sem):
    i = pl.program_id(0)
    ts = x_ref.shape[0]
    slot = i % 2

    def fetch(step, which):
        for e in range(N_EXPERTS):
            idx = step * N_EXPERTS + e
            _run_copies(run_n_ref[idx], ys_ref, run_src_ref[idx], stage.at[which],
                        run_off_ref[idx], sem.at[which], e % 2)

    @pl.when(i == 0)
    def _():
        fetch(i, slot)

    @pl.when(i < pl.num_programs(0) - 1)
    def _():
        fetch(i + 1, 1 - slot)

    pltpu.make_async_copy(ys_ref.at[pl.ds(0, 2 * ts * SEGS), :], stage.at[slot],
                          sem.at[slot]).wait()
    runs = stage.at[slot]

    def mix(c, carry):
        for u in range(ROUTE_UNROLL):
            r = c * ROUTE_UNROLL + u
            a = runs[pl.ds(pl.multiple_of(s1_ref[r], SUBLANES), SEGS), :]
            bb = runs[pl.ds(pl.multiple_of(s2_ref[r], SUBLANES), SEGS), :]
            cbuf[pl.ds(pl.multiple_of(r * SEGS, SUBLANES), SEGS), :] = (
                g1_ref[r] * a + g2_ref[r] * bb)
        return carry

    lax.fori_loop(0, ts // ROUTE_UNROLL, mix, 0)
    y = jnp.concatenate(
        [DN_ALPHA * x_ref[:, s * LANES:(s + 1) * LANES] + cbuf[pl.ds(s, ts, stride=SEGS), :]
         for s in range(SEGS)], axis=1)
    o_ref[...] = _layer_norm(y, g_ref[...], b_ref[...])


def _combine(ys, runs, sidx1_8, sidx2_8, gate1, gate2, x2, g, b):
    T = x2.shape[0]
    ts = min(ROUTE_TS, T)
    smem = lambda: pl.BlockSpec((ts,), lambda i, *_: (i,), memory_space=pltpu.SMEM)
    vec = pl.BlockSpec((1, D_MODEL), lambda i, *_: (0, 0))
    return pl.pallas_call(
        _combine_kernel,
        grid_spec=pltpu.PrefetchScalarGridSpec(
            num_scalar_prefetch=3,
            grid=(T // ts,),
            in_specs=[smem(), smem(), smem(), smem(),
                      pl.BlockSpec(memory_space=pl.ANY),
                      pl.BlockSpec((ts, D_MODEL), lambda i, *_: (i, 0)),
                      vec, vec],
            out_specs=pl.BlockSpec((ts, D_MODEL), lambda i, *_: (i, 0)),
            scratch_shapes=[pltpu.VMEM((2, 2 * ts * SEGS, LANES), F32),
                            pltpu.VMEM((ts * SEGS, LANES), F32),
                            pltpu.SemaphoreType.DMA((2,))]),
        out_shape=jax.ShapeDtypeStruct((T, D_MODEL), F32),
        compiler_params=_params("arbitrary"),
    )(runs.slot8, runs.n, runs.off8, sidx1_8, sidx2_8, gate1, gate2, ys, x2,
      g.reshape(1, -1), b.reshape(1, -1))


def _moe_block(x2, w_router, wg, wu, wd, g, b):
    T = x2.shape[0]
    tm = min(MOE_TM, T)
    ts = min(ROUTE_TS, T)
    meta_t, base, cnt = _router(x2, w_router)
    field = lambda idx: meta_t[idx].astype(jnp.int32)
    e1, e2, r1, r2 = field(META_E1), field(META_E2), field(META_R1), field(META_R2)
    counts = cnt[0, :N_EXPERTS].astype(jnp.int32)
    tiles_e = (counts + tm - 1) // tm
    padded = tiles_e * tm
    starts = jnp.cumsum(padded) - padded
    tile_end = jnp.cumsum(tiles_e)
    n_used = tile_end[-1]
    n_tiles = (TOPK_SLOTS * T) // tm + N_EXPERTS
    tid = jnp.minimum(jnp.arange(n_tiles, dtype=jnp.int32), n_used - 1)
    tile_expert = jnp.sum((tid[:, None] >= tile_end[None, :]).astype(jnp.int32), axis=1)
    before = base[:, 0, :N_EXPERTS].astype(jnp.int32)
    run_n = jnp.concatenate([before[1:], counts[None, :]], axis=0) - before
    run_off = jnp.cumsum(run_n, axis=1) - run_n
    runs = _Runs(slot8=((starts[None, :] + before) * SEGS).reshape(-1),
                 n=run_n.reshape(-1), off8=(run_off * SEGS).reshape(-1))
    to_stage = run_off - before

    def stage_row8(e, r):
        e = e.reshape(-1, ts)
        sel = sum(jnp.where(e == k, to_stage[:, k:k + 1], 0) for k in range(N_EXPERTS))
        return ((sel + r.reshape(-1, ts)) * SEGS).reshape(-1)

    sidx1_8 = stage_row8(e1, r1)
    sidx2_8 = stage_row8(e2, r2)
    pad0_8 = (starts + counts) * SEGS
    padn = padded - counts
    tail = jnp.stack([n_used * (tm * SEGS), (n_tiles - n_used) * (tm // DISP_ZB)])
    xs = _dispatch(x2, runs, sidx1_8, sidx2_8, pad0_8, padn, tail, n_tiles * tm)
    ys = _moe(xs, tile_expert, n_used.reshape(1), wg, wu, wd, tm)
    return _combine(ys, runs, sidx1_8, sidx2_8, meta_t[META_G1], meta_t[META_G2], x2, g, b)


TOPK_SLOTS = 2


def kernel(x, attn_w_qkv, attn_sink, attn_w_o, conv_pw1_w, conv_pw1_b, conv_dw_w, conv_dw_b,
           conv_norm_g, conv_norm_b, conv_pw2_w, conv_pw2_b, ffn_w_gate, ffn_w_up, ffn_w_down,
           moe_router, moe_w_gate, moe_w_up, moe_w_down, ln_mix_g, ln_mix_b, ln_ffn_g, ln_ffn_b):
    batch, seq, _ = x.shape
    x2 = x.reshape(batch * seq, D_MODEL)

    q, k, v = _qkv_rope(x2, attn_w_qkv[0].astype(BF16), seq)
    att = _attention(q, k, v, attn_sink[0], batch, seq)
    x2 = _attn_out_ffn(att, x2, attn_w_o[0].astype(BF16), ln_mix_g[0], ln_mix_b[0],
                       ffn_w_gate[0], ffn_w_up[0], ffn_w_down[0], ln_ffn_g[0], ln_ffn_b[0])

    h = _pw1_glu(x2, conv_pw1_w[0].astype(BF16), conv_pw1_b[0])
    x2 = _conv_block(h, x2, conv_dw_w[0], conv_dw_b[0], conv_norm_g[0], conv_norm_b[0],
                     conv_pw2_w[0].astype(BF16), conv_pw2_b[0], ln_mix_g[1], ln_mix_b[1],
                     batch, seq)
    x2 = _moe_block(x2, moe_router[0], moe_w_gate[0], moe_w_up[0], moe_w_down[0],
                    ln_ffn_g[1], ln_ffn_b[1])
    return x2.reshape(batch, seq, D_MODEL)
```

```python
from typing import NamedTuple

import jax
import jax.numpy as jnp
from jax import lax
from jax.experimental import pallas as pl
from jax.experimental.pallas import tpu as pltpu

F32 = jnp.float32
BF16 = jnp.bfloat16

D_MODEL = 1024
N_HEADS = 16
N_KV_HEADS = 4
HEAD_DIM = D_MODEL // N_HEADS
GROUP = N_HEADS // N_KV_HEADS
Q_DIM = N_HEADS * HEAD_DIM
KV_DIM = N_KV_HEADS * HEAD_DIM
WINDOW = 128
BLOCK = 128
ROPE_THETA = 10000.0
CONV_WIDTH = 31
CONV_HALF = CONV_WIDTH // 2
D_FF = 3584
N_EXPERTS = 8
LN_EPS = 1e-5
DEPTH = 2
DN_ALPHA = (2 * DEPTH) ** 0.25
NEG_INF = -1e30

LOG2E = 1.4426950408889634
LANES = 128
SUBLANES = 8
SEGS = D_MODEL // LANES
VAUG_DIM = N_KV_HEADS * LANES
VMEM_LIMIT = 56 * 1024 * 1024


def _params(*sem):
    return pltpu.CompilerParams(dimension_semantics=sem, vmem_limit_bytes=VMEM_LIMIT)


def _layer_norm(y, g, b):
    mu = jnp.mean(y, axis=-1, keepdims=True)
    d = y - mu
    var = jnp.mean(d * d, axis=-1, keepdims=True)
    return d * lax.rsqrt(var + LN_EPS) * g + b


def _silu(v):
    return v * jax.nn.sigmoid(v)


def _qkv_kernel(x_ref, w_ref, cos_ref, sa_ref, sb_ref, q_ref, k_ref, v_ref):
    xb = x_ref[...].astype(BF16)
    acc = jnp.dot(xb, w_ref[...], preferred_element_type=F32)
    cos = cos_ref[...]
    sa = sa_ref[...]
    sb = sb_ref[...]

    def rope(t):
        return (t * cos + pltpu.roll(t, LANES - HEAD_DIM // 2, 1) * sa
                + pltpu.roll(t, HEAD_DIM // 2, 1) * sb)

    scale = HEAD_DIM ** -0.5 * LOG2E
    for c in range(Q_DIM // LANES):
        t = acc[:, c * LANES:(c + 1) * LANES]
        q_ref[:, c * LANES:(c + 1) * LANES] = (rope(t) * scale).astype(BF16)
    for c in range(KV_DIM // LANES):
        t = acc[:, Q_DIM + c * LANES:Q_DIM + (c + 1) * LANES]
        k_ref[:, c * LANES:(c + 1) * LANES] = rope(t).astype(BF16)
    ones = jnp.ones((acc.shape[0], HEAD_DIM), F32)
    for g in range(N_KV_HEADS):
        vg = acc[:, Q_DIM + KV_DIM + g * HEAD_DIM:Q_DIM + KV_DIM + (g + 1) * HEAD_DIM]
        v_ref[:, g * LANES:(g + 1) * LANES] = jnp.concatenate([vg, ones], axis=1).astype(BF16)


def _qkv_rope(x2, w_qkv, seq, tm=1024):
    T = x2.shape[0]
    pos = jnp.arange(seq, dtype=F32)
    inv_freq = ROPE_THETA ** (-jnp.arange(0, HEAD_DIM, 2, dtype=F32) / HEAD_DIM)
    ang = pos[:, None] * inv_freq[None, :]
    cos_h, sin_h = jnp.cos(ang), jnp.sin(ang)
    zero = jnp.zeros_like(sin_h)
    reps = LANES // HEAD_DIM
    cos = jnp.tile(jnp.concatenate([cos_h, cos_h], -1), (1, reps))
    sa = jnp.tile(jnp.concatenate([-sin_h, zero], -1), (1, reps))
    sb = jnp.tile(jnp.concatenate([zero, sin_h], -1), (1, reps))
    nseq = seq // tm
    tab = pl.BlockSpec((tm, LANES), lambda i: (i % nseq, 0))
    n_out = Q_DIM + 2 * KV_DIM
    return pl.pallas_call(
        _qkv_kernel,
        grid=(T // tm,),
        in_specs=[pl.BlockSpec((tm, D_MODEL), lambda i: (i, 0)),
                  pl.BlockSpec((D_MODEL, n_out), lambda i: (0, 0)),
                  tab, tab, tab],
        out_specs=[pl.BlockSpec((tm, Q_DIM), lambda i: (i, 0)),
                   pl.BlockSpec((tm, KV_DIM), lambda i: (i, 0)),
                   pl.BlockSpec((tm, VAUG_DIM), lambda i: (i, 0))],
        out_shape=[jax.ShapeDtypeStruct((T, Q_DIM), BF16),
                   jax.ShapeDtypeStruct((T, KV_DIM), BF16),
                   jax.ShapeDtypeStruct((T, VAUG_DIM), BF16)],
        compiler_params=_params("parallel"),
    )(x2, w_qkv, cos, sa, sb)


ATT_TQ = 1024
ATT_SUB = ATT_TQ // BLOCK
ATT_PAIR = 2
ATT_KEYS = 3 * BLOCK
ATT_ROWS = GROUP * BLOCK


def _attn_kernel(sink_ref, q_ref, kp_ref, km_ref, kn_ref, vp_ref, vm_ref, vn_ref,
                 o_ref, kbuf, vbuf):
    i = pl.program_id(1)
    last = pl.num_programs(1) - 1
    kbuf[0:BLOCK] = kp_ref[...]
    kbuf[BLOCK:BLOCK + ATT_TQ] = km_ref[...]
    kbuf[BLOCK + ATT_TQ:] = kn_ref[...]
    vbuf[0:BLOCK] = vp_ref[...]
    vbuf[BLOCK:BLOCK + ATT_TQ] = vm_ref[...]
    vbuf[BLOCK + ATT_TQ:] = vn_ref[...]

    row = lax.broadcasted_iota(jnp.int32, (ATT_ROWS, BLOCK), 0)
    key = lax.broadcasted_iota(jnp.int32, (ATT_ROWS, BLOCK), 1)
    ql = row & (BLOCK - 1)
    band_prev = key >= ql
    band_next = key <= ql
    hrow = lax.broadcasted_iota(jnp.int32, (ATT_ROWS, 1), 0) // BLOCK
    sinks = []
    for g in range(N_KV_HEADS):
        sk = jnp.zeros((ATT_ROWS, 1), F32)
        for h in range(GROUP):
            sk = jnp.where(hrow == h, sink_ref[g * GROUP + h] * LOG2E, sk)
        sinks.append(sk)

    def blocks(jj, carry):
        pairs = []
        for u in range(ATT_PAIR):
            j = jj * ATT_PAIR + u
            r0 = pl.multiple_of(j * BLOCK, BLOCK)
            has_prev = jnp.logical_not((i == 0) & (j == 0))
            has_next = jnp.logical_not((i == last) & (j == ATT_SUB - 1))
            pairs += [(r0, band_prev & has_prev, band_next & has_next, g)
                      for g in range(N_KV_HEADS)]
        scores = []
        for r0, _, _, g in pairs:
            qb = q_ref[pl.ds(r0, BLOCK), g * GROUP * HEAD_DIM:(g + 1) * GROUP * HEAD_DIM]
            q4 = jnp.concatenate(
                [qb[:, h * HEAD_DIM:(h + 1) * HEAD_DIM] for h in range(GROUP)], axis=0)
            kb = kbuf[pl.ds(r0, ATT_KEYS), g * HEAD_DIM:(g + 1) * HEAD_DIM]
            scores.append(lax.dot_general(q4, kb, (((1,), (1,)), ((), ())),
                                          preferred_element_type=F32))
        probs = []
        for s, (_, valid_prev, valid_next, g) in zip(scores, pairs):
            s = jnp.concatenate(
                [jnp.where(valid_prev, s[:, :BLOCK], NEG_INF), s[:, BLOCK:2 * BLOCK],
                 jnp.where(valid_next, s[:, 2 * BLOCK:], NEG_INF)], axis=1)
            m = jnp.maximum(jnp.max(s, axis=-1, keepdims=True), sinks[g])
            probs.append((jnp.exp2(s - m).astype(BF16), jnp.exp2(sinks[g] - m)))
        for (p, p_sink), (r0, _, _, g) in zip(probs, pairs):
            vb = vbuf[pl.ds(r0, ATT_KEYS), g * LANES:(g + 1) * LANES]
            oa = jnp.dot(p, vb, preferred_element_type=F32)
            o = oa[:, :HEAD_DIM] / (oa[:, HEAD_DIM:] + p_sink)
            o_ref[pl.ds(r0, BLOCK), g * GROUP * HEAD_DIM:(g + 1) * GROUP * HEAD_DIM] = (
                jnp.concatenate([o[h * BLOCK:(h + 1) * BLOCK] for h in range(GROUP)],
                                axis=1).astype(BF16))
        return carry

    lax.fori_loop(0, ATT_SUB // ATT_PAIR, blocks, 0)


def _attention(q, k, v, sink, batch, seq):
    nb = seq // BLOCK
    q3 = q.reshape(batch, seq, Q_DIM)
    k3 = k.reshape(batch, seq, KV_DIM)
    v3 = v.reshape(batch, seq, VAUG_DIM)
    prev = lambda w: pl.BlockSpec((None, BLOCK, w),
                                  lambda b, i, s: (b, jnp.maximum(i * ATT_SUB - 1, 0), 0))
    main = lambda w: pl.BlockSpec((None, ATT_TQ, w), lambda b, i, s: (b, i, 0))
    nxt = lambda w: pl.BlockSpec(
        (None, BLOCK, w), lambda b, i, s: (b, jnp.minimum((i + 1) * ATT_SUB, nb - 1), 0))
    out = pl.pallas_call(
        _attn_kernel,
        grid_spec=pltpu.PrefetchScalarGridSpec(
            num_scalar_prefetch=1,
            grid=(batch, seq // ATT_TQ),
            in_specs=[pl.BlockSpec((None, ATT_TQ, Q_DIM), lambda b, i, s: (b, i, 0)),
                      prev(KV_DIM), main(KV_DIM), nxt(KV_DIM),
                      prev(VAUG_DIM), main(VAUG_DIM), nxt(VAUG_DIM)],
            out_specs=pl.BlockSpec((None, ATT_TQ, Q_DIM), lambda b, i, s: (b, i, 0)),
            scratch_shapes=[pltpu.VMEM((ATT_TQ + 2 * BLOCK, KV_DIM), BF16),
                            pltpu.VMEM((ATT_TQ + 2 * BLOCK, VAUG_DIM), BF16)]),
        out_shape=jax.ShapeDtypeStruct((batch, seq, Q_DIM), BF16),
        compiler_params=_params("parallel", "parallel"),
    )(sink.astype(F32), q3, k3, k3, k3, v3, v3, v3)
    return out.reshape(batch * seq, Q_DIM)


FFN_FC = 512
FFN_TM = 1024
FFN_RC = 256


def _swiglu_hidden(xb, wg_ref, wu_ref):
    gate = jnp.dot(xb, wg_ref[...].astype(BF16), preferred_element_type=F32)
    up = jnp.dot(xb, wu_ref[...].astype(BF16), preferred_element_type=F32)
    return (_silu(gate) * up).astype(BF16)


def _swiglu_step(xb, wg_ref, wu_ref, wd_ref):
    h = _swiglu_hidden(xb, wg_ref, wu_ref)
    return jnp.dot(h, wd_ref[...].astype(BF16), preferred_element_type=F32)


def _attn_out_ffn_kernel(att_ref, x_ref, wo_ref, g1_ref, b1_ref, wg_ref, wu_ref, wd_ref,
                         g2_ref, b2_ref, o_ref, x1_scr, xb_scr, acc_scr):
    k = pl.program_id(1)

    @pl.when(k == 0)
    def _():
        for r in range(0, att_ref.shape[0], FFN_RC):
            rows = slice(r, r + FFN_RC)
            y = jnp.dot(att_ref[rows, :], wo_ref[...], preferred_element_type=F32)
            x1 = _layer_norm(DN_ALPHA * x_ref[rows, :] + y, g1_ref[...], b1_ref[...])
            x1_scr[rows, :] = x1
            xb_scr[rows, :] = x1.astype(BF16)
        acc_scr[...] = jnp.zeros_like(acc_scr)

    last = pl.num_programs(1) - 1

    @pl.when(k < last)
    def _():
        acc_scr[...] += _swiglu_step(xb_scr[...], wg_ref, wu_ref, wd_ref)

    @pl.when(k == last)
    def _():
        h = _swiglu_hidden(xb_scr[...], wg_ref, wu_ref)
        wd = wd_ref[...].astype(BF16)
        for r in range(0, h.shape[0], FFN_RC):
            rows = slice(r, r + FFN_RC)
            ffn = acc_scr[rows, :] + jnp.dot(h[rows, :], wd, preferred_element_type=F32)
            y = DN_ALPHA * x1_scr[rows, :] + ffn
            o_ref[rows, :] = _layer_norm(y, g2_ref[...], b2_ref[...])


def _attn_out_ffn(att, x2, wo, g1, b1, wg, wu, wd, g2, b2):
    T = x2.shape[0]
    tm, fc = min(FFN_TM, T), FFN_FC
    vec = pl.BlockSpec((1, D_MODEL), lambda i, k: (0, 0))
    row = pl.BlockSpec((tm, D_MODEL), lambda i, k: (i, 0))
    return pl.pallas_call(
        _attn_out_ffn_kernel,
        grid=(T // tm, D_FF // fc),
        in_specs=[row, row,
                  pl.BlockSpec((D_MODEL, D_MODEL), lambda i, k: (0, 0)),
                  vec, vec,
                  pl.BlockSpec((D_MODEL, fc), lambda i, k: (0, k)),
                  pl.BlockSpec((D_MODEL, fc), lambda i, k: (0, k)),
                  pl.BlockSpec((fc, D_MODEL), lambda i, k: (k, 0)),
                  vec, vec],
        out_specs=row,
        out_shape=jax.ShapeDtypeStruct((T, D_MODEL), F32),
        scratch_shapes=[pltpu.VMEM((tm, D_MODEL), F32), pltpu.VMEM((tm, D_MODEL), BF16),
                        pltpu.VMEM((tm, D_MODEL), F32)],
        compiler_params=_params("parallel", "arbitrary"),
    )(att, x2, wo, g1.reshape(1, -1), b1.reshape(1, -1), wg, wu, wd,
      g2.reshape(1, -1), b2.reshape(1, -1))


def _pw1_glu_kernel(x_ref, w_ref, bias_ref, o_ref):
    h = jnp.dot(x_ref[...].astype(BF16), w_ref[...], preferred_element_type=F32) + bias_ref[...]
    o_ref[...] = h[:, :D_MODEL] * jax.nn.sigmoid(h[:, D_MODEL:])


def _pw1_glu(x2, w, bias, tm=1024):
    T = x2.shape[0]
    return pl.pallas_call(
        _pw1_glu_kernel,
        grid=(T // tm,),
        in_specs=[pl.BlockSpec((tm, D_MODEL), lambda i: (i, 0)),
                  pl.BlockSpec((D_MODEL, 2 * D_MODEL), lambda i: (0, 0)),
                  pl.BlockSpec((1, 2 * D_MODEL), lambda i: (0, 0))],
        out_specs=pl.BlockSpec((tm, D_MODEL), lambda i: (i, 0)),
        out_shape=jax.ShapeDtypeStruct((T, D_MODEL), F32),
        compiler_params=_params("parallel"),
    )(x2, w, bias.reshape(1, -1))


CONV_TM = 512
CONV_HALO = 16
CONV_RC = 64
CONV_PARTS = 2
CONV_SHIFT_ROWS = CONV_TM + 2 * CONV_HALO - SUBLANES


def _conv_kernel(hp_ref, hm_ref, hn_ref, x_ref, dw_ref, dwb_ref, ng_ref, nb_ref,
                 w2_ref, b2_ref, g_ref, b_ref, o_ref, hbuf, hshift, cbuf):
    i = pl.program_id(1)
    last = pl.num_programs(1) - 1
    tm = CONV_TM
    hbuf[0:CONV_HALO] = jnp.where(i > 0, hp_ref[...], 0.0)
    hbuf[CONV_HALO:CONV_HALO + tm] = hm_ref[...]
    hbuf[CONV_HALO + tm:] = jnp.where(i < last, hn_ref[...], 0.0)
    for sh in range(1, SUBLANES):
        hshift[sh - 1] = hbuf[sh:sh + CONV_SHIFT_ROWS, :]
    off = CONV_HALO - CONV_HALF

    def rows(r, carry):
        r0 = pl.multiple_of(r * CONV_RC, CONV_RC)
        for c in range(SEGS):
            cols = slice(c * LANES, (c + 1) * LANES)
            parts = [None] * CONV_PARTS
            for sh in range(SUBLANES):
                taps = [w for w in range(CONV_WIDTH) if (off + w) % SUBLANES == sh]
                src = hbuf if sh == 0 else hshift.at[sh - 1]
                span = (off + taps[-1]) // SUBLANES * SUBLANES + CONV_RC
                seg = src[pl.ds(r0, span), cols]
                for w in taps:
                    lo = (off + w) // SUBLANES * SUBLANES
                    term = seg[lo:lo + CONV_RC] * dw_ref[w:w + 1, cols]
                    k = w % CONV_PARTS
                    parts[k] = term if parts[k] is None else parts[k] + term
            acc = parts[0]
            for part in parts[1:]:
                acc = acc + part
            cbuf[pl.ds(r0, CONV_RC), cols] = acc + dwb_ref[:, cols]
        return carry

    lax.fori_loop(0, tm // CONV_RC, rows, 0)
    h = _silu(_layer_norm(cbuf[...], ng_ref[...], nb_ref[...]))
    y = jnp.dot(h.astype(BF16), w2_ref[...], preferred_element_type=F32) + b2_ref[...]
    y = DN_ALPHA * x_ref[...] + y
    o_ref[...] = _layer_norm(y, g_ref[...], b_ref[...])


def _conv_block(h, x2, dw_w, dw_b, ng, nb, w2, b2, g, b, batch, seq):
    tm = CONV_TM
    h3 = h.reshape(batch, seq, D_MODEL)
    x3 = x2.reshape(batch, seq, D_MODEL)
    per = tm // CONV_HALO
    nh = seq // CONV_HALO
    vec = pl.BlockSpec((1, D_MODEL), lambda bb, i: (0, 0))
    out = pl.pallas_call(
        _conv_kernel,
        grid=(batch, seq // tm),
        in_specs=[pl.BlockSpec((None, CONV_HALO, D_MODEL),
                               lambda bb, i: (bb, jnp.maximum(i * per - 1, 0), 0)),
                  pl.BlockSpec((None, tm, D_MODEL), lambda bb, i: (bb, i, 0)),
                  pl.BlockSpec((None, CONV_HALO, D_MODEL),
                               lambda bb, i: (bb, jnp.minimum((i + 1) * per, nh - 1), 0)),
                  pl.BlockSpec((None, tm, D_MODEL), lambda bb, i: (bb, i, 0)),
                  pl.BlockSpec((CONV_WIDTH, D_MODEL), lambda bb, i: (0, 0)),
                  vec, vec, vec,
                  pl.BlockSpec((D_MODEL, D_MODEL), lambda bb, i: (0, 0)),
                  vec, vec, vec],
        out_specs=pl.BlockSpec((None, tm, D_MODEL), lambda bb, i: (bb, i, 0)),
        out_shape=jax.ShapeDtypeStruct((batch, seq, D_MODEL), F32),
        scratch_shapes=[pltpu.VMEM((tm + 2 * CONV_HALO, D_MODEL), F32),
                        pltpu.VMEM((SUBLANES - 1, CONV_SHIFT_ROWS, D_MODEL), F32),
                        pltpu.VMEM((tm, D_MODEL), F32)],
        compiler_params=_params("parallel", "parallel"),
    )(h3, h3, h3, x3, dw_w, dw_b.reshape(1, -1), ng.reshape(1, -1), nb.reshape(1, -1),
      w2, b2.reshape(1, -1), g.reshape(1, -1), b.reshape(1, -1))
    return out.reshape(batch * seq, D_MODEL)


ROUTE_TS = 512
ROUTE_BITS = ROUTE_TS.bit_length()
META_E1, META_E2, META_R1, META_R2, META_G1, META_G2 = range(6)


def _router_kernel(x_ref, w_ref, meta_t_ref, base_ref, cnt_ref, base_scr):
    t = pl.program_id(0)

    @pl.when(t == 0)
    def _():
        base_scr[...] = jnp.zeros_like(base_scr)

    base_ref[...] = base_scr[...]

    tm = x_ref.shape[0]
    x = x_ref[...]
    xh = x.astype(BF16)
    xl = (x - xh.astype(F32)).astype(BF16)
    parts = jnp.dot(jnp.concatenate([xh, xl], axis=0), w_ref[...],
                    preferred_element_type=F32)
    logits = ((parts[:tm, :LANES] + parts[:tm, LANES:])
              + (parts[tm:, :LANES] + parts[tm:, LANES:]))
    lane = lax.broadcasted_iota(jnp.int32, (tm, LANES), 1)
    lane_f = lane.astype(F32)
    lg = jnp.where(lane < N_EXPERTS, logits, -jnp.inf)
    m1 = jnp.max(lg, axis=-1, keepdims=True)
    i1 = jnp.min(jnp.where(lg == m1, lane_f, float(LANES)), axis=-1, keepdims=True)
    oh1 = lane_f == i1
    lg2 = jnp.where(oh1, -jnp.inf, lg)
    m2 = jnp.max(lg2, axis=-1, keepdims=True)
    i2 = jnp.min(jnp.where(lg2 == m2, lane_f, float(LANES)), axis=-1, keepdims=True)
    oh2 = lane_f == i2
    e = jnp.exp(m2 - m1)
    g1 = 1.0 / (1.0 + e)
    g2 = e / (1.0 + e)
    oh = jnp.where(oh1 | oh2, 1.0, 0.0)
    r = lax.broadcasted_iota(jnp.int32, (tm, tm), 0)
    c = lax.broadcasted_iota(jnp.int32, (tm, tm), 1)
    lower = jnp.where(c < r, 1.0, 0.0).astype(BF16)
    before = jnp.dot(lower, oh.astype(BF16), preferred_element_type=F32) + base_scr[...]
    rank1 = jnp.sum(jnp.where(oh1, before, 0.0), axis=-1, keepdims=True)
    rank2 = jnp.sum(jnp.where(oh2, before, 0.0), axis=-1, keepdims=True)
    base_scr[...] += jnp.sum(oh, axis=0, keepdims=True)
    meta = jnp.zeros((tm, LANES), F32)
    for idx, val in ((META_E1, i1), (META_E2, i2),
                     (META_R1, rank1), (META_R2, rank2), (META_G1, g1), (META_G2, g2)):
        meta = jnp.where(lane == idx, val, meta)
    meta_t_ref[...] = jnp.transpose(meta)[:SUBLANES]
    cnt_ref[...] = base_scr[...]


def _router(x2, w_router):
    T = x2.shape[0]
    tm = min(ROUTE_TS, T)
    w = jnp.zeros((D_MODEL, LANES), F32).at[:, :N_EXPERTS].set(w_router.astype(F32))
    w_hi = w.astype(BF16)
    w_lo = (w - w_hi.astype(F32)).astype(BF16)
    w = jnp.concatenate([w_hi, w_lo], axis=1)
    return pl.pallas_call(
        _router_kernel,
        grid=(T // tm,),
        in_specs=[pl.BlockSpec((tm, D_MODEL), lambda i: (i, 0)),
                  pl.BlockSpec((D_MODEL, 2 * LANES), lambda i: (0, 0))],
        out_specs=[pl.BlockSpec((SUBLANES, tm), lambda i: (0, i)),
                   pl.BlockSpec((None, 1, LANES), lambda i: (i, 0, 0)),
                   pl.BlockSpec((1, LANES), lambda i: (0, 0))],
        out_shape=[jax.ShapeDtypeStruct((SUBLANES, T), F32),
                   jax.ShapeDtypeStruct((T // tm, 1, LANES), F32),
                   jax.ShapeDtypeStruct((1, LANES), F32)],
        scratch_shapes=[pltpu.VMEM((1, LANES), F32)],
        compiler_params=_params("arbitrary"),
    )(x2, w)


DISP_ZB = 128


class _Runs(NamedTuple):
    slot8: jax.Array
    n: jax.Array
    off8: jax.Array


def _row_copy(src, src_row8, dst, dst_row8, sem):
    return pltpu.make_async_copy(
        src.at[pl.ds(pl.multiple_of(src_row8, SUBLANES), SEGS), :],
        dst.at[pl.ds(pl.multiple_of(dst_row8, SUBLANES), SEGS), :], sem)


def _run_copies(n, src, src0_8, dst, dst0_8, sem, priority):
    for bit in range(ROUTE_BITS):
        size = 1 << bit

        @pl.when((n & size) != 0)
        def _():
            above8 = ((n >> (bit + 1)) << (bit + 1)) * SEGS
            pltpu.make_async_copy(
                src.at[pl.ds(pl.multiple_of(src0_8 + above8, SUBLANES), size * SEGS), :],
                dst.at[pl.ds(pl.multiple_of(dst0_8 + above8, SUBLANES), size * SEGS), :],
                sem).start(priority=priority)


ROUTE_UNROLL = 8


def _dispatch_kernel(pad0_ref, padn_ref, tail_ref, run_dst_ref, run_n_ref, run_off_ref,
                     s1_ref, s2_ref, x_ref, xs_ref, sbuf, stage, zbuf, sem, zsem):
    i = pl.program_id(0)
    ts = x_ref.shape[0]
    slot = i % 2
    runs = stage.at[slot]
    for s in range(SEGS):
        sbuf[pl.ds(s, ts, stride=SEGS), :] = x_ref[:, s * LANES:(s + 1) * LANES]

    def place(c, carry):
        for u in range(ROUTE_UNROLL):
            r = c * ROUTE_UNROLL + u
            tile = sbuf[pl.ds(pl.multiple_of(r * SEGS, SUBLANES), SEGS), :]
            runs[pl.ds(pl.multiple_of(s1_ref[r], SUBLANES), SEGS), :] = tile
            runs[pl.ds(pl.multiple_of(s2_ref[r], SUBLANES), SEGS), :] = tile
        return carry

    lax.fori_loop(0, ts // ROUTE_UNROLL, place, 0)
    for e in range(N_EXPERTS):
        idx = i * N_EXPERTS + e
        _run_copies(run_n_ref[idx], runs, run_off_ref[idx], xs_ref, run_dst_ref[idx],
                    sem.at[slot], e % 2)

    def wait_step(which):
        pltpu.make_async_copy(stage.at[which], xs_ref.at[pl.ds(0, 2 * ts * SEGS), :],
                              sem.at[which]).wait()

    @pl.when(pl.program_id(0) == 0)
    def _():
        zbuf[...] = jnp.zeros_like(zbuf)
        for e in range(N_EXPERTS):
            def zstart(j, carry, e=e):
                _row_copy(zbuf, 0, xs_ref, pad0_ref[e] + j * SEGS, zsem).start()
                return carry

            def zwait(j, carry, e=e):
                _row_copy(zbuf, 0, xs_ref, pad0_ref[e] + j * SEGS, zsem).wait()
                return carry

            lax.fori_loop(0, padn_ref[e], zstart, 0)
            lax.fori_loop(0, padn_ref[e], zwait, 0)

        def tail_copy(j):
            rows = DISP_ZB * SEGS
            dst0 = pl.multiple_of(tail_ref[0] + j * rows, SUBLANES)
            return pltpu.make_async_copy(zbuf, xs_ref.at[pl.ds(dst0, rows), :], zsem)

        def tstart(j, carry):
            tail_copy(j).start()
            return carry

        def twait(j, carry):
            tail_copy(j).wait()
            return carry

        lax.fori_loop(0, tail_ref[1], tstart, 0)
        lax.fori_loop(0, tail_ref[1], twait, 0)

    @pl.when(i > 0)
    def _():
        wait_step(1 - slot)

    @pl.when(i == pl.num_programs(0) - 1)
    def _():
        wait_step(slot)


def _dispatch(x2, runs, sidx1_8, sidx2_8, pad0_8, padn, tail, n_slots):
    T = x2.shape[0]
    ts = min(ROUTE_TS, T)
    smem = lambda: pl.BlockSpec((ts,), lambda i, *_: (i,), memory_space=pltpu.SMEM)
    return pl.pallas_call(
        _dispatch_kernel,
        grid_spec=pltpu.PrefetchScalarGridSpec(
            num_scalar_prefetch=6,
            grid=(T // ts,),
            in_specs=[smem(), smem(),
                      pl.BlockSpec((ts, D_MODEL), lambda i, *_: (i, 0))],
            out_specs=pl.BlockSpec(memory_space=pl.ANY),
            scratch_shapes=[pltpu.VMEM((ts * SEGS, LANES), F32),
                            pltpu.VMEM((2, 2 * ts * SEGS, LANES), F32),
                            pltpu.VMEM((DISP_ZB * SEGS, LANES), F32),
                            pltpu.SemaphoreType.DMA((2,)),
                            pltpu.SemaphoreType.DMA(())]),
        out_shape=jax.ShapeDtypeStruct((n_slots * SEGS, LANES), F32),
        compiler_params=_params("arbitrary"),
    )(pad0_8, padn, tail, runs.slot8, runs.n, runs.off8, sidx1_8, sidx2_8, x2)


MOE_TM = 1024
MOE_FC = 512
MOE_RC = 256


def _moe_kernel(te_ref, nu_ref, xs_ref, wg_ref, wu_ref, wd_ref, o_ref, xb_scr, acc_scr):
    i = pl.program_id(0)
    k = pl.program_id(1)
    tm = xb_scr.shape[0]

    @pl.when(i < nu_ref[0])
    def _():
        @pl.when(k == 0)
        def _():
            for s in range(SEGS):
                xb_scr[:, s * LANES:(s + 1) * LANES] = (
                    xs_ref[pl.ds(s, tm, stride=SEGS), :].astype(BF16))
            acc_scr[...] = jnp.zeros_like(acc_scr)

        last = pl.num_programs(1) - 1

        @pl.when(k < last)
        def _():
            acc_scr[...] += _swiglu_step(xb_scr[...], wg_ref, wu_ref, wd_ref)

        @pl.when(k == last)
        def _():
            h = _swiglu_hidden(xb_scr[...], wg_ref, wu_ref)
            wd = wd_ref[...].astype(BF16)
            rc = min(MOE_RC, tm)
            for r in range(0, tm, rc):
                rows = slice(r, r + rc)
                y = acc_scr[rows, :] + jnp.dot(h[rows, :], wd, preferred_element_type=F32)
                for s in range(SEGS):
                    o_ref[pl.ds(r * SEGS + s, rc, stride=SEGS), :] = (
                        y[:, s * LANES:(s + 1) * LANES])

    @pl.when((i >= nu_ref[0]) & (k == 0))
    def _():
        o_ref[...] = jnp.zeros_like(o_ref)


def _moe(xs, tile_expert, n_used, wg, wu, wd, tm):
    n_tiles = xs.shape[0] // (tm * SEGS)
    fc = MOE_FC
    nk = D_FF // fc

    def tile(i, k, te, nu):
        return (jnp.minimum(i, nu[0] - 1), 0)

    def kk(i, k, nu):
        return jnp.where(i < nu[0], k, nk - 1)

    return pl.pallas_call(
        _moe_kernel,
        grid_spec=pltpu.PrefetchScalarGridSpec(
            num_scalar_prefetch=2,
            grid=(n_tiles, nk),
            in_specs=[pl.BlockSpec((tm * SEGS, LANES), tile),
                      pl.BlockSpec((None, D_MODEL, fc),
                                   lambda i, k, te, nu: (te[i], 0, kk(i, k, nu))),
                      pl.BlockSpec((None, D_MODEL, fc),
                                   lambda i, k, te, nu: (te[i], 0, kk(i, k, nu))),
                      pl.BlockSpec((None, fc, D_MODEL),
                                   lambda i, k, te, nu: (te[i], kk(i, k, nu), 0))],
            out_specs=pl.BlockSpec((tm * SEGS, LANES), lambda i, k, te, nu: (i, 0)),
            scratch_shapes=[pltpu.VMEM((tm, D_MODEL), BF16), pltpu.VMEM((tm, D_MODEL), F32)]),
        out_shape=jax.ShapeDtypeStruct(xs.shape, F32),
        compiler_params=_params("arbitrary", "arbitrary"),
    )(tile_expert, n_used, xs, wg, wu, wd)


def _combine_kernel(run_src_ref, run_n_ref, run_off_ref, s1_ref, s2_ref, g1_ref, g2_ref,
                    ys_ref, x_ref, g_ref, b_ref, o_ref, stage, cbuf, sem):
    i = pl.program_id(0)
    ts = x_ref.shape[0]
    slot = i % 2

    def fetch(step, which):
        for e in range(N_EXPERTS):
            idx = step * N_EXPERTS + e
            _run_copies(run_n_ref[idx], ys_ref, run_src_ref[idx], stage.at[which],
                        run_off_ref[idx], sem.at[which], e % 2)

    @pl.when(i == 0)
    def _():
        fetch(i, slot)

    @pl.when(i < pl.num_programs(0) - 1)
    def _():
        fetch(i + 1, 1 - slot)

    pltpu.make_async_copy(ys_ref.at[pl.ds(0, 2 * ts * SEGS), :], stage.at[slot],
                          sem.at[slot]).wait()
    runs = stage.at[slot]

    def mix(c, carry):
        for u in range(ROUTE_UNROLL):
            r = c * ROUTE_UNROLL + u
            a = runs[pl.ds(pl.multiple_of(s1_ref[r], SUBLANES), SEGS), :]
            bb = runs[pl.ds(pl.multiple_of(s2_ref[r], SUBLANES), SEGS), :]
            cbuf[pl.ds(pl.multiple_of(r * SEGS, SUBLANES), SEGS), :] = (
                g1_ref[r] * a + g2_ref[r] * bb)
        return carry

    lax.fori_loop(0, ts // ROUTE_UNROLL, mix, 0)
    y = jnp.concatenate(
        [DN_ALPHA * x_ref[:, s * LANES:(s + 1) * LANES] + cbuf[pl.ds(s, ts, stride=SEGS), :]
         for s in range(SEGS)], axis=1)
    o_ref[...] = _layer_norm(y, g_ref[...], b_ref[...])


def _combine(ys, runs, sidx1_8, sidx2_8, gate1, gate2, x2, g, b):
    T = x2.shape[0]
    ts = min(ROUTE_TS, T)
    smem = lambda: pl.BlockSpec((ts,), lambda i, *_: (i,), memory_space=pltpu.SMEM)
    vec = pl.BlockSpec((1, D_MODEL), lambda i, *_: (0, 0))
    return pl.pallas_call(
        _combine_kernel,
        grid_spec=pltpu.PrefetchScalarGridSpec(
            num_scalar_prefetch=3,
            grid=(T // ts,),
            in_specs=[smem(), smem(), smem(), smem(),
                      pl.BlockSpec(memory_space=pl.ANY),
                      pl.BlockSpec((ts, D_MODEL), lambda i, *_: (i, 0)),
                      vec, vec],
            out_specs=pl.BlockSpec((ts, D_MODEL), lambda i, *_: (i, 0)),
            scratch_shapes=[pltpu.VMEM((2, 2 * ts * SEGS, LANES), F32),
                            pltpu.VMEM((ts * SEGS, LANES), F32),
                            pltpu.SemaphoreType.DMA((2,))]),
        out_shape=jax.ShapeDtypeStruct((T, D_MODEL), F32),
        compiler_params=_params("arbitrary"),
    )(runs.slot8, runs.n, runs.off8, sidx1_8, sidx2_8, gate1, gate2, ys, x2,
      g.reshape(1, -1), b.reshape(1, -1))


def _moe_block(x2, w_router, wg, wu, wd, g, b):
    T = x2.shape[0]
    tm = min(MOE_TM, T)
    ts = min(ROUTE_TS, T)
    meta_t, base, cnt = _router(x2, w_router)
    field = lambda idx: meta_t[idx].astype(jnp.int32)
    e1, e2, r1, r2 = field(META_E1), field(META_E2), field(META_R1), field(META_R2)
    counts = cnt[0, :N_EXPERTS].astype(jnp.int32)
    tiles_e = (counts + tm - 1) // tm
    padded = tiles_e * tm
    starts = jnp.cumsum(padded) - padded
    tile_end = jnp.cumsum(tiles_e)
    n_used = tile_end[-1]
    n_tiles = (TOPK_SLOTS * T) // tm + N_EXPERTS
    tid = jnp.minimum(jnp.arange(n_tiles, dtype=jnp.int32), n_used - 1)
    tile_expert = jnp.sum((tid[:, None] >= tile_end[None, :]).astype(jnp.int32), axis=1)
    before = base[:, 0, :N_EXPERTS].astype(jnp.int32)
    run_n = jnp.concatenate([before[1:], counts[None, :]], axis=0) - before
    run_off = jnp.cumsum(run_n, axis=1) - run_n
    runs = _Runs(slot8=((starts[None, :] + before) * SEGS).reshape(-1),
                 n=run_n.reshape(-1), off8=(run_off * SEGS).reshape(-1))
    to_stage = run_off - before

    def stage_row8(e, r):
        e = e.reshape(-1, ts)
        sel = sum(jnp.where(e == k, to_stage[:, k:k + 1], 0) for k in range(N_EXPERTS))
        return ((sel + r.reshape(-1, ts)) * SEGS).reshape(-1)

    sidx1_8 = stage_row8(e1, r1)
    sidx2_8 = stage_row8(e2, r2)
    pad0_8 = (starts + counts) * SEGS
    padn = padded - counts
    tail = jnp.stack([n_used * (tm * SEGS), (n_tiles - n_used) * (tm // DISP_ZB)])
    xs = _dispatch(x2, runs, sidx1_8, sidx2_8, pad0_8, padn, tail, n_tiles * tm)
    ys = _moe(xs, tile_expert, n_used.reshape(1), wg, wu, wd, tm)
    return _combine(ys, runs, sidx1_8, sidx2_8, meta_t[META_G1], meta_t[META_G2], x2, g, b)


TOPK_SLOTS = 2


def kernel(x, attn_w_qkv, attn_sink, attn_w_o, conv_pw1_w, conv_pw1_b, conv_dw_w, conv_dw_b,
           conv_norm_g, conv_norm_b, conv_pw2_w, conv_pw2_b, ffn_w_gate, ffn_w_up, ffn_w_down,
           moe_router, moe_w_gate, moe_w_up, moe_w_down, ln_mix_g, ln_mix_b, ln_ffn_g, ln_ffn_b):
    batch, seq, _ = x.shape
    x2 = x.reshape(batch * seq, D_MODEL)

    q, k, v = _qkv_rope(x2, attn_w_qkv[0].astype(BF16), seq)
    att = _attention(q, k, v, attn_sink[0], batch, seq)
    x2 = _attn_out_ffn(att, x2, attn_w_o[0].astype(BF16), ln_mix_g[0], ln_mix_b[0],
                       ffn_w_gate[0], ffn_w_up[0], ffn_w_down[0], ln_ffn_g[0], ln_ffn_b[0])

    h = _pw1_glu(x2, conv_pw1_w[0].astype(BF16), conv_pw1_b[0])
    x2 = _conv_block(h, x2, conv_dw_w[0], conv_dw_b[0], conv_norm_g[0], conv_norm_b[0],
                     conv_pw2_w[0].astype(BF16), conv_pw2_b[0], ln_mix_g[1], ln_mix_b[1],
                     batch, seq)
    x2 = _moe_block(x2, moe_router[0], moe_w_gate[0], moe_w_up[0], moe_w_down[0],
                    ln_ffn_g[1], ln_ffn_b[1])
    return x2.reshape(batch, seq, D_MODEL)
```

```python
from typing import NamedTuple

import jax
import jax.numpy as jnp
from jax import lax
from jax.experimental import pallas as pl
from jax.experimental.pallas import tpu as pltpu

F32 = jnp.float32
BF16 = jnp.bfloat16

D_MODEL = 1024
N_HEADS = 16
N_KV_HEADS = 4
HEAD_DIM = D_MODEL // N_HEADS
GROUP = N_HEADS // N_KV_HEADS
Q_DIM = N_HEADS * HEAD_DIM
KV_DIM = N_KV_HEADS * HEAD_DIM
WINDOW = 128
BLOCK = 128
ROPE_THETA = 10000.0
CONV_WIDTH = 31
CONV_HALF = CONV_WIDTH // 2
D_FF = 3584
N_EXPERTS = 8
LN_EPS = 1e-5
DEPTH = 2
DN_ALPHA = (2 * DEPTH) ** 0.25
NEG_INF = -1e30

LOG2E = 1.4426950408889634
LANES = 128
SUBLANES = 8
SEGS = D_MODEL // LANES
VAUG_DIM = N_KV_HEADS * LANES
VMEM_LIMIT = 56 * 1024 * 1024


def _params(*sem):
    return pltpu.CompilerParams(dimension_semantics=sem, vmem_limit_bytes=VMEM_LIMIT)


def _layer_norm(y, g, b):
    mu = jnp.mean(y, axis=-1, keepdims=True)
    d = y - mu
    var = jnp.mean(d * d, axis=-1, keepdims=True)
    return d * lax.rsqrt(var + LN_EPS) * g + b


def _silu(v):
    return v * jax.nn.sigmoid(v)


def _qkv_kernel(x_ref, w_ref, cos_ref, sa_ref, sb_ref, q_ref, k_ref, v_ref):
    xb = x_ref[...].astype(BF16)
    acc = jnp.dot(xb, w_ref[...], preferred_element_type=F32)
    cos = cos_ref[...]
    sa = sa_ref[...]
    sb = sb_ref[...]

    def rope(t):
        return (t * cos + pltpu.roll(t, LANES - HEAD_DIM // 2, 1) * sa
                + pltpu.roll(t, HEAD_DIM // 2, 1) * sb)

    scale = HEAD_DIM ** -0.5 * LOG2E
    for c in range(Q_DIM // LANES):
        t = acc[:, c * LANES:(c + 1) * LANES]
        q_ref[:, c * LANES:(c + 1) * LANES] = (rope(t) * scale).astype(BF16)
    for c in range(KV_DIM // LANES):
        t = acc[:, Q_DIM + c * LANES:Q_DIM + (c + 1) * LANES]
        k_ref[:, c * LANES:(c + 1) * LANES] = rope(t).astype(BF16)
    ones = jnp.ones((acc.shape[0], HEAD_DIM), F32)
    for g in range(N_KV_HEADS):
        vg = acc[:, Q_DIM + KV_DIM + g * HEAD_DIM:Q_DIM + KV_DIM + (g + 1) * HEAD_DIM]
        v_ref[:, g * LANES:(g + 1) * LANES] = jnp.concatenate([vg, ones], axis=1).astype(BF16)


def _qkv_rope(x2, w_qkv, seq, tm=1024):
    T = x2.shape[0]
    pos = jnp.arange(seq, dtype=F32)
    inv_freq = ROPE_THETA ** (-jnp.arange(0, HEAD_DIM, 2, dtype=F32) / HEAD_DIM)
    ang = pos[:, None] * inv_freq[None, :]
    cos_h, sin_h = jnp.cos(ang), jnp.sin(ang)
    zero = jnp.zeros_like(sin_h)
    reps = LANES // HEAD_DIM
    cos = jnp.tile(jnp.concatenate([cos_h, cos_h], -1), (1, reps))
    sa = jnp.tile(jnp.concatenate([-sin_h, zero], -1), (1, reps))
    sb = jnp.tile(jnp.concatenate([zero, sin_h], -1), (1, reps))
    nseq = seq // tm
    tab = pl.BlockSpec((tm, LANES), lambda i: (i % nseq, 0))
    n_out = Q_DIM + 2 * KV_DIM
    return pl.pallas_call(
        _qkv_kernel,
        grid=(T // tm,),
        in_specs=[pl.BlockSpec((tm, D_MODEL), lambda i: (i, 0)),
                  pl.BlockSpec((D_MODEL, n_out), lambda i: (0, 0)),
                  tab, tab, tab],
        out_specs=[pl.BlockSpec((tm, Q_DIM), lambda i: (i, 0)),
                   pl.BlockSpec((tm, KV_DIM), lambda i: (i, 0)),
                   pl.BlockSpec((tm, VAUG_DIM), lambda i: (i, 0))],
        out_shape=[jax.ShapeDtypeStruct((T, Q_DIM), BF16),
                   jax.ShapeDtypeStruct((T, KV_DIM), BF16),
                   jax.ShapeDtypeStruct((T, VAUG_DIM), BF16)],
        compiler_params=_params("parallel"),
    )(x2, w_qkv, cos, sa, sb)


ATT_TQ = 1024
ATT_SUB = ATT_TQ // BLOCK
ATT_PAIR = 2
ATT_KEYS = 3 * BLOCK
ATT_ROWS = GROUP * BLOCK


def _attn_kernel(sink_ref, q_ref, kp_ref, km_ref, kn_ref, vp_ref, vm_ref, vn_ref,
                 o_ref, kbuf, vbuf):
    i = pl.program_id(1)
    last = pl.num_programs(1) - 1
    kbuf[0:BLOCK] = kp_ref[...]
    kbuf[BLOCK:BLOCK + ATT_TQ] = km_ref[...]
    kbuf[BLOCK + ATT_TQ:] = kn_ref[...]
    vbuf[0:BLOCK] = vp_ref[...]
    vbuf[BLOCK:BLOCK + ATT_TQ] = vm_ref[...]
    vbuf[BLOCK + ATT_TQ:] = vn_ref[...]

    row = lax.broadcasted_iota(jnp.int32, (ATT_ROWS, BLOCK), 0)
    key = lax.broadcasted_iota(jnp.int32, (ATT_ROWS, BLOCK), 1)
    ql = row & (BLOCK - 1)
    band_prev = key >= ql
    band_next = key <= ql
    hrow = lax.broadcasted_iota(jnp.int32, (ATT_ROWS, 1), 0) // BLOCK
    sinks = []
    for g in range(N_KV_HEADS):
        sk = jnp.zeros((ATT_ROWS, 1), F32)
        for h in range(GROUP):
            sk = jnp.where(hrow == h, sink_ref[g * GROUP + h] * LOG2E, sk)
        sinks.append(sk)

    def blocks(jj, carry):
        pairs = []
        for u in range(ATT_PAIR):
            j = jj * ATT_PAIR + u
            r0 = pl.multiple_of(j * BLOCK, BLOCK)
            has_prev = jnp.logical_not((i == 0) & (j == 0))
            has_next = jnp.logical_not((i == last) & (j == ATT_SUB - 1))
            pairs += [(r0, band_prev & has_prev, band_next & has_next, g)
                      for g in range(N_KV_HEADS)]
        scores = []
        for r0, _, _, g in pairs:
            qb = q_ref[pl.ds(r0, BLOCK), g * GROUP * HEAD_DIM:(g + 1) * GROUP * HEAD_DIM]
            q4 = jnp.concatenate(
                [qb[:, h * HEAD_DIM:(h + 1) * HEAD_DIM] for h in range(GROUP)], axis=0)
            kb = kbuf[pl.ds(r0, ATT_KEYS), g * HEAD_DIM:(g + 1) * HEAD_DIM]
            scores.append(lax.dot_general(q4, kb, (((1,), (1,)), ((), ())),
                                          preferred_element_type=F32))
        probs = []
        for s, (_, valid_prev, valid_next, g) in zip(scores, pairs):
            s = jnp.concatenate(
                [jnp.where(valid_prev, s[:, :BLOCK], NEG_INF), s[:, BLOCK:2 * BLOCK],
                 jnp.where(valid_next, s[:, 2 * BLOCK:], NEG_INF)], axis=1)
            m = jnp.maximum(jnp.max(s, axis=-1, keepdims=True), sinks[g])
            probs.append((jnp.exp2(s - m).astype(BF16), jnp.exp2(sinks[g] - m)))
        for (p, p_sink), (r0, _, _, g) in zip(probs, pairs):
            vb = vbuf[pl.ds(r0, ATT_KEYS), g * LANES:(g + 1) * LANES]
            oa = jnp.dot(p, vb, preferred_element_type=F32)
            o = oa[:, :HEAD_DIM] / (oa[:, HEAD_DIM:] + p_sink)
            o_ref[pl.ds(r0, BLOCK), g * GROUP * HEAD_DIM:(g + 1) * GROUP * HEAD_DIM] = (
                jnp.concatenate([o[h * BLOCK:(h + 1) * BLOCK] for h in range(GROUP)],
                                axis=1).astype(BF16))
        return carry

    lax.fori_loop(0, ATT_SUB // ATT_PAIR, blocks, 0)


def _attention(q, k, v, sink, batch, seq):
    nb = seq // BLOCK
    q3 = q.reshape(batch, seq, Q_DIM)
    k3 = k.reshape(batch, seq, KV_DIM)
    v3 = v.reshape(batch, seq, VAUG_DIM)
    prev = lambda w: pl.BlockSpec((None, BLOCK, w),
                                  lambda b, i, s: (b, jnp.maximum(i * ATT_SUB - 1, 0), 0))
    main = lambda w: pl.BlockSpec((None, ATT_TQ, w), lambda b, i, s: (b, i, 0))
    nxt = lambda w: pl.BlockSpec(
        (None, BLOCK, w), lambda b, i, s: (b, jnp.minimum((i + 1) * ATT_SUB, nb - 1), 0))
    out = pl.pallas_call(
        _attn_kernel,
        grid_spec=pltpu.PrefetchScalarGridSpec(
            num_scalar_prefetch=1,
            grid=(batch, seq // ATT_TQ),
            in_specs=[pl.BlockSpec((None, ATT_TQ, Q_DIM), lambda b, i, s: (b, i, 0)),
                      prev(KV_DIM), main(KV_DIM), nxt(KV_DIM),
                      prev(VAUG_DIM), main(VAUG_DIM), nxt(VAUG_DIM)],
            out_specs=pl.BlockSpec((None, ATT_TQ, Q_DIM), lambda b, i, s: (b, i, 0)),
            scratch_shapes=[pltpu.VMEM((ATT_TQ + 2 * BLOCK, KV_DIM), BF16),
                            pltpu.VMEM((ATT_TQ + 2 * BLOCK, VAUG_DIM), BF16)]),
        out_shape=jax.ShapeDtypeStruct((batch, seq, Q_DIM), BF16),
        compiler_params=_params("parallel", "parallel"),
    )(sink.astype(F32), q3, k3, k3, k3, v3, v3, v3)
    return out.reshape(batch * seq, Q_DIM)


FFN_FC = 512
FFN_TM = 1024
FFN_RC = 256


def _swiglu_hidden(xb, wg_ref, wu_ref):
    gate = jnp.dot(xb, wg_ref[...].astype(BF16), preferred_element_type=F32)
    up = jnp.dot(xb, wu_ref[...].astype(BF16), preferred_element_type=F32)
    return (_silu(gate) * up).astype(BF16)


def _swiglu_step(xb, wg_ref, wu_ref, wd_ref):
    h = _swiglu_hidden(xb, wg_ref, wu_ref)
    return jnp.dot(h, wd_ref[...].astype(BF16), preferred_element_type=F32)


def _attn_out_ffn_kernel(att_ref, x_ref, wo_ref, g1_ref, b1_ref, wg_ref, wu_ref, wd_ref,
                         g2_ref, b2_ref, o_ref, x1_scr, xb_scr, acc_scr):
    k = pl.program_id(1)

    @pl.when(k == 0)
    def _():
        for r in range(0, att_ref.shape[0], FFN_RC):
            rows = slice(r, r + FFN_RC)
            y = jnp.dot(att_ref[rows, :], wo_ref[...], preferred_element_type=F32)
            x1 = _layer_norm(DN_ALPHA * x_ref[rows, :] + y, g1_ref[...], b1_ref[...])
            x1_scr[rows, :] = x1
            xb_scr[rows, :] = x1.astype(BF16)
        acc_scr[...] = jnp.zeros_like(acc_scr)

    last = pl.num_programs(1) - 1

    @pl.when(k < last)
    def _():
        acc_scr[...] += _swiglu_step(xb_scr[...], wg_ref, wu_ref, wd_ref)

    @pl.when(k == last)
    def _():
        h = _swiglu_hidden(xb_scr[...], wg_ref, wu_ref)
        wd = wd_ref[...].astype(BF16)
        for r in range(0, h.shape[0], FFN_RC):
            rows = slice(r, r + FFN_RC)
            ffn = acc_scr[rows, :] + jnp.dot(h[rows, :], wd, preferred_element_type=F32)
            y = DN_ALPHA * x1_scr[rows, :] + ffn
            o_ref[rows, :] = _layer_norm(y, g2_ref[...], b2_ref[...])


def _attn_out_ffn(att, x2, wo, g1, b1, wg, wu, wd, g2, b2):
    T = x2.shape[0]
    tm, fc = min(FFN_TM, T), FFN_FC
    vec = pl.BlockSpec((1, D_MODEL), lambda i, k: (0, 0))
    row = pl.BlockSpec((tm, D_MODEL), lambda i, k: (i, 0))
    return pl.pallas_call(
        _attn_out_ffn_kernel,
        grid=(T // tm, D_FF // fc),
        in_specs=[row, row,
                  pl.BlockSpec((D_MODEL, D_MODEL), lambda i, k: (0, 0)),
                  vec, vec,
                  pl.BlockSpec((D_MODEL, fc), lambda i, k: (0, k)),
                  pl.BlockSpec((D_MODEL, fc), lambda i, k: (0, k)),
                  pl.BlockSpec((fc, D_MODEL), lambda i, k: (k, 0)),
                  vec, vec],
        out_specs=row,
        out_shape=jax.ShapeDtypeStruct((T, D_MODEL), F32),
        scratch_shapes=[pltpu.VMEM((tm, D_MODEL), F32), pltpu.VMEM((tm, D_MODEL), BF16),
                        pltpu.VMEM((tm, D_MODEL), F32)],
        compiler_params=_params("parallel", "arbitrary"),
    )(att, x2, wo, g1.reshape(1, -1), b1.reshape(1, -1), wg, wu, wd,
      g2.reshape(1, -1), b2.reshape(1, -1))


def _pw1_glu_kernel(x_ref, w_ref, bias_ref, o_ref):
    h = jnp.dot(x_ref[...].astype(BF16), w_ref[...], preferred_element_type=F32) + bias_ref[...]
    o_ref[...] = h[:, :D_MODEL] * jax.nn.sigmoid(h[:, D_MODEL:])


def _pw1_glu(x2, w, bias, tm=1024):
    T = x2.shape[0]
    return pl.pallas_call(
        _pw1_glu_kernel,
        grid=(T // tm,),
        in_specs=[pl.BlockSpec((tm, D_MODEL), lambda i: (i, 0)),
                  pl.BlockSpec((D_MODEL, 2 * D_MODEL), lambda i: (0, 0)),
                  pl.BlockSpec((1, 2 * D_MODEL), lambda i: (0, 0))],
        out_specs=pl.BlockSpec((tm, D_MODEL), lambda i: (i, 0)),
        out_shape=jax.ShapeDtypeStruct((T, D_MODEL), F32),
        compiler_params=_params("parallel"),
    )(x2, w, bias.reshape(1, -1))


CONV_TM = 512
CONV_HALO = 16
CONV_RC = 64
CONV_PARTS = 2
CONV_SHIFT_ROWS = CONV_TM + 2 * CONV_HALO - SUBLANES


def _conv_kernel(hp_ref, hm_ref, hn_ref, x_ref, dw_ref, dwb_ref, ng_ref, nb_ref,
                 w2_ref, b2_ref, g_ref, b_ref, o_ref, hbuf, hshift, cbuf):
    i = pl.program_id(1)
    last = pl.num_programs(1) - 1
    tm = CONV_TM
    hbuf[0:CONV_HALO] = jnp.where(i > 0, hp_ref[...], 0.0)
    hbuf[CONV_HALO:CONV_HALO + tm] = hm_ref[...]
    hbuf[CONV_HALO + tm:] = jnp.where(i < last, hn_ref[...], 0.0)
    for sh in range(1, SUBLANES):
        hshift[sh - 1] = hbuf[sh:sh + CONV_SHIFT_ROWS, :]
    off = CONV_HALO - CONV_HALF

    def rows(r, carry):
        r0 = pl.multiple_of(r * CONV_RC, CONV_RC)
        for c in range(SEGS):
            cols = slice(c * LANES, (c + 1) * LANES)
            parts = [None] * CONV_PARTS
            for sh in range(SUBLANES):
                taps = [w for w in range(CONV_WIDTH) if (off + w) % SUBLANES == sh]
                src = hbuf if sh == 0 else hshift.at[sh - 1]
                span = (off + taps[-1]) // SUBLANES * SUBLANES + CONV_RC
                seg = src[pl.ds(r0, span), cols]
                for w in taps:
                    lo = (off + w) // SUBLANES * SUBLANES
                    term = seg[lo:lo + CONV_RC] * dw_ref[w:w + 1, cols]
                    k = w % CONV_PARTS
                    parts[k] = term if parts[k] is None else parts[k] + term
            acc = parts[0]
            for part in parts[1:]:
                acc = acc + part
            cbuf[pl.ds(r0, CONV_RC), cols] = acc + dwb_ref[:, cols]
        return carry

    lax.fori_loop(0, tm // CONV_RC, rows, 0)
    h = _silu(_layer_norm(cbuf[...], ng_ref[...], nb_ref[...]))
    y = jnp.dot(h.astype(BF16), w2_ref[...], preferred_element_type=F32) + b2_ref[...]
    y = DN_ALPHA * x_ref[...] + y
    o_ref[...] = _layer_norm(y, g_ref[...], b_ref[...])


def _conv_block(h, x2, dw_w, dw_b, ng, nb, w2, b2, g, b, batch, seq):
    tm = CONV_TM
    h3 = h.reshape(batch, seq, D_MODEL)
    x3 = x2.reshape(batch, seq, D_MODEL)
    per = tm // CONV_HALO
    nh = seq // CONV_HALO
    vec = pl.BlockSpec((1, D_MODEL), lambda bb, i: (0, 0))
    out = pl.pallas_call(
        _conv_kernel,
        grid=(batch, seq // tm),
        in_specs=[pl.BlockSpec((None, CONV_HALO, D_MODEL),
                               lambda bb, i: (bb, jnp.maximum(i * per - 1, 0), 0)),
                  pl.BlockSpec((None, tm, D_MODEL), lambda bb, i: (bb, i, 0)),
                  pl.BlockSpec((None, CONV_HALO, D_MODEL),
                               lambda bb, i: (bb, jnp.minimum((i + 1) * per, nh - 1), 0)),
                  pl.BlockSpec((None, tm, D_MODEL), lambda bb, i: (bb, i, 0)),
                  pl.BlockSpec((CONV_WIDTH, D_MODEL), lambda bb, i: (0, 0)),
                  vec, vec, vec,
                  pl.BlockSpec((D_MODEL, D_MODEL), lambda bb, i: (0, 0)),
                  vec, vec, vec],
        out_specs=pl.BlockSpec((None, tm, D_MODEL), lambda bb, i: (bb, i, 0)),
        out_shape=jax.ShapeDtypeStruct((batch, seq, D_MODEL), F32),
        scratch_shapes=[pltpu.VMEM((tm + 2 * CONV_HALO, D_MODEL), F32),
                        pltpu.VMEM((SUBLANES - 1, CONV_SHIFT_ROWS, D_MODEL), F32),
                        pltpu.VMEM((tm, D_MODEL), F32)],
        compiler_params=_params("parallel", "parallel"),
    )(h3, h3, h3, x3, dw_w, dw_b.reshape(1, -1), ng.reshape(1, -1), nb.reshape(1, -1),
      w2, b2.reshape(1, -1), g.reshape(1, -1), b.reshape(1, -1))
    return out.reshape(batch * seq, D_MODEL)


ROUTE_TS = 512
ROUTE_BITS = ROUTE_TS.bit_length()
META_E1, META_E2, META_R1, META_R2, META_G1, META_G2 = range(6)


def _router_kernel(x_ref, w_ref, meta_t_ref, base_ref, cnt_ref, base_scr):
    t = pl.program_id(0)

    @pl.when(t == 0)
    def _():
        base_scr[...] = jnp.zeros_like(base_scr)

    base_ref[...] = base_scr[...]

    tm = x_ref.shape[0]
    x = x_ref[...]
    xh = x.astype(BF16)
    xl = (x - xh.astype(F32)).astype(BF16)
    parts = jnp.dot(jnp.concatenate([xh, xl], axis=0), w_ref[...],
                    preferred_element_type=F32)
    logits = ((parts[:tm, :LANES] + parts[:tm, LANES:])
              + (parts[tm:, :LANES] + parts[tm:, LANES:]))
    lane = lax.broadcasted_iota(jnp.int32, (tm, LANES), 1)
    lane_f = lane.astype(F32)
    lg = jnp.where(lane < N_EXPERTS, logits, -jnp.inf)
    m1 = jnp.max(lg, axis=-1, keepdims=True)
    i1 = jnp.min(jnp.where(lg == m1, lane_f, float(LANES)), axis=-1, keepdims=True)
    oh1 = lane_f == i1
    lg2 = jnp.where(oh1, -jnp.inf, lg)
    m2 = jnp.max(lg2, axis=-1, keepdims=True)
    i2 = jnp.min(jnp.where(lg2 == m2, lane_f, float(LANES)), axis=-1, keepdims=True)
    oh2 = lane_f == i2
    e = jnp.exp(m2 - m1)
    g1 = 1.0 / (1.0 + e)
    g2 = e / (1.0 + e)
    oh = jnp.where(oh1 | oh2, 1.0, 0.0)
    r = lax.broadcasted_iota(jnp.int32, (tm, tm), 0)
    c = lax.broadcasted_iota(jnp.int32, (tm, tm), 1)
    lower = jnp.where(c < r, 1.0, 0.0).astype(BF16)
    before = jnp.dot(lower, oh.astype(BF16), preferred_element_type=F32) + base_scr[...]
    rank1 = jnp.sum(jnp.where(oh1, before, 0.0), axis=-1, keepdims=True)
    rank2 = jnp.sum(jnp.where(oh2, before, 0.0), axis=-1, keepdims=True)
    base_scr[...] += jnp.sum(oh, axis=0, keepdims=True)
    meta = jnp.zeros((tm, LANES), F32)
    for idx, val in ((META_E1, i1), (META_E2, i2),
                     (META_R1, rank1), (META_R2, rank2), (META_G1, g1), (META_G2, g2)):
        meta = jnp.where(lane == idx, val, meta)
    meta_t_ref[...] = jnp.transpose(meta)[:SUBLANES]
    cnt_ref[...] = base_scr[...]


def _router(x2, w_router):
    T = x2.shape[0]
    tm = min(ROUTE_TS, T)
    w = jnp.zeros((D_MODEL, LANES), F32).at[:, :N_EXPERTS].set(w_router.astype(F32))
    w_hi = w.astype(BF16)
    w_lo = (w - w_hi.astype(F32)).astype(BF16)
    w = jnp.concatenate([w_hi, w_lo], axis=1)
    return pl.pallas_call(
        _router_kernel,
        grid=(T // tm,),
        in_specs=[pl.BlockSpec((tm, D_MODEL), lambda i: (i, 0)),
                  pl.BlockSpec((D_MODEL, 2 * LANES), lambda i: (0, 0))],
        out_specs=[pl.BlockSpec((SUBLANES, tm), lambda i: (0, i)),
                   pl.BlockSpec((None, 1, LANES), lambda i: (i, 0, 0)),
                   pl.BlockSpec((1, LANES), lambda i: (0, 0))],
        out_shape=[jax.ShapeDtypeStruct((SUBLANES, T), F32),
                   jax.ShapeDtypeStruct((T // tm, 1, LANES), F32),
                   jax.ShapeDtypeStruct((1, LANES), F32)],
        scratch_shapes=[pltpu.VMEM((1, LANES), F32)],
        compiler_params=_params("arbitrary"),
    )(x2, w)


DISP_ZB = 128


class _Runs(NamedTuple):
    slot8: jax.Array
    n: jax.Array
    off8: jax.Array


def _row_copy(src, src_row8, dst, dst_row8, sem):
    return pltpu.make_async_copy(
        src.at[pl.ds(pl.multiple_of(src_row8, SUBLANES), SEGS), :],
        dst.at[pl.ds(pl.multiple_of(dst_row8, SUBLANES), SEGS), :], sem)


def _run_copies(n, src, src0_8, dst, dst0_8, sem, priority):
    for bit in range(ROUTE_BITS):
        size = 1 << bit

        @pl.when((n & size) != 0)
        def _():
            above8 = ((n >> (bit + 1)) << (bit + 1)) * SEGS
            pltpu.make_async_copy(
                src.at[pl.ds(pl.multiple_of(src0_8 + above8, SUBLANES), size * SEGS), :],
                dst.at[pl.ds(pl.multiple_of(dst0_8 + above8, SUBLANES), size * SEGS), :],
                sem).start(priority=priority)


ROUTE_UNROLL = 8


def _dispatch_kernel(pad0_ref, padn_ref, tail_ref, run_dst_ref, run_n_ref, run_off_ref,
                     s1_ref, s2_ref, x_ref, xs_ref, sbuf, stage, zbuf, sem, zsem):
    i = pl.program_id(0)
    ts = x_ref.shape[0]
    slot = i % 2
    runs = stage.at[slot]
    for s in range(SEGS):
        sbuf[pl.ds(s, ts, stride=SEGS), :] = x_ref[:, s * LANES:(s + 1) * LANES]

    def place(c, carry):
        for u in range(ROUTE_UNROLL):
            r = c * ROUTE_UNROLL + u
            tile = sbuf[pl.ds(pl.multiple_of(r * SEGS, SUBLANES), SEGS), :]
            runs[pl.ds(pl.multiple_of(s1_ref[r], SUBLANES), SEGS), :] = tile
            runs[pl.ds(pl.multiple_of(s2_ref[r], SUBLANES), SEGS), :] = tile
        return carry

    lax.fori_loop(0, ts // ROUTE_UNROLL, place, 0)
    for e in range(N_EXPERTS):
        idx = i * N_EXPERTS + e
        _run_copies(run_n_ref[idx], runs, run_off_ref[idx], xs_ref, run_dst_ref[idx],
                    sem.at[slot], e % 2)

    def wait_step(which):
        pltpu.make_async_copy(stage.at[which], xs_ref.at[pl.ds(0, 2 * ts * SEGS), :],
                              sem.at[which]).wait()

    @pl.when(pl.program_id(0) == 0)
    def _():
        zbuf[...] = jnp.zeros_like(zbuf)
        for e in range(N_EXPERTS):
            def zstart(j, carry, e=e):
                _row_copy(zbuf, 0, xs_ref, pad0_ref[e] + j * SEGS, zsem).start()
                return carry

            def zwait(j, carry, e=e):
                _row_copy(zbuf, 0, xs_ref, pad0_ref[e] + j * SEGS, zsem).wait()
                return carry

            lax.fori_loop(0, padn_ref[e], zstart, 0)
            lax.fori_loop(0, padn_ref[e], zwait, 0)

        def tail_copy(j):
            rows = DISP_ZB * SEGS
            dst0 = pl.multiple_of(tail_ref[0] + j * rows, SUBLANES)
            return pltpu.make_async_copy(zbuf, xs_ref.at[pl.ds(dst0, rows), :], zsem)

        def tstart(j, carry):
            tail_copy(j).start()
            return carry

        def twait(j, carry):
            tail_copy(j).wait()
            return carry

        lax.fori_loop(0, tail_ref[1], tstart, 0)
        lax.fori_loop(0, tail_ref[1], twait, 0)

    @pl.when(i > 0)
    def _():
        wait_step(1 - slot)

    @pl.when(i == pl.num_programs(0) - 1)
    def _():
        wait_step(slot)


def _dispatch(x2, runs, sidx1_8, sidx2_8, pad0_8, padn, tail, n_slots):
    T = x2.shape[0]
    ts = min(ROUTE_TS, T)
    smem = lambda: pl.BlockSpec((ts,), lambda i, *_: (i,), memory_space=pltpu.SMEM)
    return pl.pallas_call(
        _dispatch_kernel,
        grid_spec=pltpu.PrefetchScalarGridSpec(
            num_scalar_prefetch=6,
            grid=(T // ts,),
            in_specs=[smem(), smem(),
                      pl.BlockSpec((ts, D_MODEL), lambda i, *_: (i, 0))],
            out_specs=pl.BlockSpec(memory_space=pl.ANY),
            scratch_shapes=[pltpu.VMEM((ts * SEGS, LANES), F32),
                            pltpu.VMEM((2, 2 * ts * SEGS, LANES), F32),
                            pltpu.VMEM((DISP_ZB * SEGS, LANES), F32),
                            pltpu.SemaphoreType.DMA((2,)),
                            pltpu.SemaphoreType.DMA(())]),
        out_shape=jax.ShapeDtypeStruct((n_slots * SEGS, LANES), F32),
        compiler_params=_params("arbitrary"),
    )(pad0_8, padn, tail, runs.slot8, runs.n, runs.off8, sidx1_8, sidx2_8, x2)


MOE_TM = 1024
MOE_FC = 512
MOE_RC = 256


def _moe_kernel(te_ref, nu_ref, xs_ref, wg_ref, wu_ref, wd_ref, o_ref, xb_scr, acc_scr):
    i = pl.program_id(0)
    k = pl.program_id(1)
    tm = xb_scr.shape[0]

    @pl.when(i < nu_ref[0])
    def _():
        last = pl.num_programs(1) - 1
        rc = min(MOE_RC, tm)

        @pl.when(k == 0)
        def _():
            wg = wg_ref[...].astype(BF16)
            wu = wu_ref[...].astype(BF16)
            wd = wd_ref[...].astype(BF16)
            for r in range(0, tm, rc):
                rows = slice(r, r + rc)
                xb = jnp.concatenate(
                    [xs_ref[pl.ds(r * SEGS + s, rc, stride=SEGS), :] for s in range(SEGS)],
                    axis=1).astype(BF16)
                xb_scr[rows, :] = xb
                gate = jnp.dot(xb, wg, preferred_element_type=F32)
                up = jnp.dot(xb, wu, preferred_element_type=F32)
                h = (_silu(gate) * up).astype(BF16)
                acc_scr[rows, :] = jnp.dot(h, wd, preferred_element_type=F32)

        @pl.when((k > 0) & (k < last))
        def _():
            acc_scr[...] += _swiglu_step(xb_scr[...], wg_ref, wu_ref, wd_ref)

        @pl.when(k == last)
        def _():
            h = _swiglu_hidden(xb_scr[...], wg_ref, wu_ref)
            wd = wd_ref[...].astype(BF16)
            for r in range(0, tm, rc):
                rows = slice(r, r + rc)
                y = acc_scr[rows, :] + jnp.dot(h[rows, :], wd, preferred_element_type=F32)
                for s in range(SEGS):
                    o_ref[pl.ds(r * SEGS + s, rc, stride=SEGS), :] = (
                        y[:, s * LANES:(s + 1) * LANES])

    @pl.when((i >= nu_ref[0]) & (k == 0))
    def _():
        o_ref[...] = jnp.zeros_like(o_ref)


def _moe(xs, tile_expert, n_used, wg, wu, wd, tm):
    n_tiles = xs.shape[0] // (tm * SEGS)
    fc = MOE_FC
    nk = D_FF // fc

    def tile(i, k, te, nu):
        return (jnp.minimum(i, nu[0] - 1), 0)

    def kk(i, k, nu):
        return jnp.where(i < nu[0], k, nk - 1)

    return pl.pallas_call(
        _moe_kernel,
        grid_spec=pltpu.PrefetchScalarGridSpec(
            num_scalar_prefetch=2,
            grid=(n_tiles, nk),
            in_specs=[pl.BlockSpec((tm * SEGS, LANES), tile),
                      pl.BlockSpec((None, D_MODEL, fc),
                                   lambda i, k, te, nu: (te[i], 0, kk(i, k, nu))),
                      pl.BlockSpec((None, D_MODEL, fc),
                                   lambda i, k, te, nu: (te[i], 0, kk(i, k, nu))),
                      pl.BlockSpec((None, fc, D_MODEL),
                                   lambda i, k, te, nu: (te[i], kk(i, k, nu), 0))],
            out_specs=pl.BlockSpec((tm * SEGS, LANES), lambda i, k, te, nu: (i, 0)),
            scratch_shapes=[pltpu.VMEM((tm, D_MODEL), BF16), pltpu.VMEM((tm, D_MODEL), F32)]),
        out_shape=jax.ShapeDtypeStruct(xs.shape, F32),
        compiler_params=_params("arbitrary", "arbitrary"),
    )(tile_expert, n_used, xs, wg, wu, wd)


def _combine_kernel(run_src_ref, run_n_ref, run_off_ref, s1_ref, s2_ref, g1_ref, g2_ref,
                    ys_ref, x_ref, g_ref, b_ref, o_ref, stage, cbuf, sem):
    i = pl.program_id(0)
    ts = x_ref.shape[0]
    slot = i % 2

    def fetch(step, which):
        for e in range(N_EXPERTS):
            idx = step * N_EXPERTS + e
            _run_copies(run_n_ref[idx], ys_ref, run_src_ref[idx], stage.at[which],
                        run_off_ref[idx], sem.at[which], e % 2)

    @pl.when(i == 0)
    def _():
        fetch(i, slot)

    @pl.when(i < pl.num_programs(0) - 1)
    def _():
        fetch(i + 1, 1 - slot)

    pltpu.make_async_copy(ys_ref.at[pl.ds(0, 2 * ts * SEGS), :], stage.at[slot],
                          sem.at[slot]).wait()
    runs = stage.at[slot]

    def mix(c, carry):
        for u in range(ROUTE_UNROLL):
            r = c * ROUTE_UNROLL + u
            a = runs[pl.ds(pl.multiple_of(s1_ref[r], SUBLANES), SEGS), :]
            bb = runs[pl.ds(pl.multiple_of(s2_ref[r], SUBLANES), SEGS), :]
            cbuf[pl.ds(pl.multiple_of(r * SEGS, SUBLANES), SEGS), :] = (
                g1_ref[r] * a + g2_ref[r] * bb)
        return carry

    lax.fori_loop(0, ts // ROUTE_UNROLL, mix, 0)
    y = jnp.concatenate(
        [DN_ALPHA * x_ref[:, s * LANES:(s + 1) * LANES] + cbuf[pl.ds(s, ts, stride=SEGS), :]
         for s in range(SEGS)], axis=1)
    o_ref[...] = _layer_norm(y, g_ref[...], b_ref[...])


def _combine(ys, runs, sidx1_8, sidx2_8, gate1, gate2, x2, g, b):
    T = x2.shape[0]
    ts = min(ROUTE_TS, T)
    smem = lambda: pl.BlockSpec((ts,), lambda i, *_: (i,), memory_space=pltpu.SMEM)
    vec = pl.BlockSpec((1, D_MODEL), lambda i, *_: (0, 0))
    return pl.pallas_call(
        _combine_kernel,
        grid_spec=pltpu.PrefetchScalarGridSpec(
            num_scalar_prefetch=3,
            grid=(T // ts,),
            in_specs=[smem(), smem(), smem(), smem(),
                      pl.BlockSpec(memory_space=pl.ANY),
                      pl.BlockSpec((ts, D_MODEL), lambda i, *_: (i, 0)),
                      vec, vec],
            out_specs=pl.BlockSpec((ts, D_MODEL), lambda i, *_: (i, 0)),
            scratch_shapes=[pltpu.VMEM((2, 2 * ts * SEGS, LANES), F32),
                            pltpu.VMEM((ts * SEGS, LANES), F32),
                            pltpu.SemaphoreType.DMA((2,))]),
        out_shape=jax.ShapeDtypeStruct((T, D_MODEL), F32),
        compiler_params=_params("arbitrary"),
    )(runs.slot8, runs.n, runs.off8, sidx1_8, sidx2_8, gate1, gate2, ys, x2,
      g.reshape(1, -1), b.reshape(1, -1))


def _moe_block(x2, w_router, wg, wu, wd, g, b):
    T = x2.shape[0]
    tm = min(MOE_TM, T)
    ts = min(ROUTE_TS, T)
    meta_t, base, cnt = _router(x2, w_router)
    field = lambda idx: meta_t[idx].astype(jnp.int32)
    e1, e2, r1, r2 = field(META_E1), field(META_E2), field(META_R1), field(META_R2)
    counts = cnt[0, :N_EXPERTS].astype(jnp.int32)
    tiles_e = (counts + tm - 1) // tm
    padded = tiles_e * tm
    starts = jnp.cumsum(padded) - padded
    tile_end = jnp.cumsum(tiles_e)
    n_used = tile_end[-1]
    n_tiles = (TOPK_SLOTS * T) // tm + N_EXPERTS
    tid = jnp.minimum(jnp.arange(n_tiles, dtype=jnp.int32), n_used - 1)
    tile_expert = jnp.sum((tid[:, None] >= tile_end[None, :]).astype(jnp.int32), axis=1)
    before = base[:, 0, :N_EXPERTS].astype(jnp.int32)
    run_n = jnp.concatenate([before[1:], counts[None, :]], axis=0) - before
    run_off = jnp.cumsum(run_n, axis=1) - run_n
    runs = _Runs(slot8=((starts[None, :] + before) * SEGS).reshape(-1),
                 n=run_n.reshape(-1), off8=(run_off * SEGS).reshape(-1))
    to_stage = run_off - before

    def stage_row8(e, r):
        e = e.reshape(-1, ts)
        sel = sum(jnp.where(e == k, to_stage[:, k:k + 1], 0) for k in range(N_EXPERTS))
        return ((sel + r.reshape(-1, ts)) * SEGS).reshape(-1)

    sidx1_8 = stage_row8(e1, r1)
    sidx2_8 = stage_row8(e2, r2)
    pad0_8 = (starts + counts) * SEGS
    padn = padded - counts
    tail = jnp.stack([n_used * (tm * SEGS), (n_tiles - n_used) * (tm // DISP_ZB)])
    xs = _dispatch(x2, runs, sidx1_8, sidx2_8, pad0_8, padn, tail, n_tiles * tm)
    ys = _moe(xs, tile_expert, n_used.reshape(1), wg, wu, wd, tm)
    return _combine(ys, runs, sidx1_8, sidx2_8, meta_t[META_G1], meta_t[META_G2], x2, g, b)


TOPK_SLOTS = 2


def kernel(x, attn_w_qkv, attn_sink, attn_w_o, conv_pw1_w, conv_pw1_b, conv_dw_w, conv_dw_b,
           conv_norm_g, conv_norm_b, conv_pw2_w, conv_pw2_b, ffn_w_gate, ffn_w_up, ffn_w_down,
           moe_router, moe_w_gate, moe_w_up, moe_w_down, ln_mix_g, ln_mix_b, ln_ffn_g, ln_ffn_b):
    batch, seq, _ = x.shape
    x2 = x.reshape(batch * seq, D_MODEL)

    q, k, v = _qkv_rope(x2, attn_w_qkv[0].astype(BF16), seq)
    att = _attention(q, k, v, attn_sink[0], batch, seq)
    x2 = _attn_out_ffn(att, x2, attn_w_o[0].astype(BF16), ln_mix_g[0], ln_mix_b[0],
                       ffn_w_gate[0], ffn_w_up[0], ffn_w_down[0], ln_ffn_g[0], ln_ffn_b[0])

    h = _pw1_glu(x2, conv_pw1_w[0].astype(BF16), conv_pw1_b[0])
    x2 = _conv_block(h, x2, conv_dw_w[0], conv_dw_b[0], conv_norm_g[0], conv_norm_b[0],
                     conv_pw2_w[0].astype(BF16), conv_pw2_b[0], ln_mix_g[1], ln_mix_b[1],
                     batch, seq)
    x2 = _moe_block(x2, moe_router[0], moe_w_gate[0], moe_w_up[0], moe_w_down[0],
                    ln_ffn_g[1], ln_ffn_b[1])
    return x2.reshape(batch, seq, D_MODEL)
```

```python
from typing import NamedTuple

import jax
import jax.numpy as jnp
from jax import lax
from jax.experimental import pallas as pl
from jax.experimental.pallas import tpu as pltpu

F32 = jnp.float32
BF16 = jnp.bfloat16

D_MODEL = 1024
N_HEADS = 16
N_KV_HEADS = 4
HEAD_DIM = D_MODEL // N_HEADS
GROUP = N_HEADS // N_KV_HEADS
Q_DIM = N_HEADS * HEAD_DIM
KV_DIM = N_KV_HEADS * HEAD_DIM
WINDOW = 128
BLOCK = 128
ROPE_THETA = 10000.0
CONV_WIDTH = 31
CONV_HALF = CONV_WIDTH // 2
D_FF = 3584
N_EXPERTS = 8
LN_EPS = 1e-5
DEPTH = 2
DN_ALPHA = (2 * DEPTH) ** 0.25
NEG_INF = -1e30

LOG2E = 1.4426950408889634
LANES = 128
SUBLANES = 8
SEGS = D_MODEL // LANES
VAUG_DIM = N_KV_HEADS * LANES
VMEM_LIMIT = 56 * 1024 * 1024


def _params(*sem):
    return pltpu.CompilerParams(dimension_semantics=sem, vmem_limit_bytes=VMEM_LIMIT)


def _layer_norm(y, g, b):
    mu = jnp.mean(y, axis=-1, keepdims=True)
    d = y - mu
    var = jnp.mean(d * d, axis=-1, keepdims=True)
    return d * lax.rsqrt(var + LN_EPS) * g + b


def _silu(v):
    return v * jax.nn.sigmoid(v)


def _qkv_kernel(x_ref, w_ref, cos_ref, sa_ref, sb_ref, q_ref, k_ref, v_ref):
    xb = x_ref[...].astype(BF16)
    acc = jnp.dot(xb, w_ref[...], preferred_element_type=F32)
    cos = cos_ref[...]
    sa = sa_ref[...]
    sb = sb_ref[...]

    def rope(t):
        return (t * cos + pltpu.roll(t, LANES - HEAD_DIM // 2, 1) * sa
                + pltpu.roll(t, HEAD_DIM // 2, 1) * sb)

    scale = HEAD_DIM ** -0.5 * LOG2E
    for c in range(Q_DIM // LANES):
        t = acc[:, c * LANES:(c + 1) * LANES]
        q_ref[:, c * LANES:(c + 1) * LANES] = (rope(t) * scale).astype(BF16)
    for c in range(KV_DIM // LANES):
        t = acc[:, Q_DIM + c * LANES:Q_DIM + (c + 1) * LANES]
        k_ref[:, c * LANES:(c + 1) * LANES] = rope(t).astype(BF16)
    ones = jnp.ones((acc.shape[0], HEAD_DIM), F32)
    for g in range(N_KV_HEADS):
        vg = acc[:, Q_DIM + KV_DIM + g * HEAD_DIM:Q_DIM + KV_DIM + (g + 1) * HEAD_DIM]
        v_ref[:, g * LANES:(g + 1) * LANES] = jnp.concatenate([vg, ones], axis=1).astype(BF16)


def _qkv_rope(x2, w_qkv, seq, tm=1024):
    T = x2.shape[0]
    pos = jnp.arange(seq, dtype=F32)
    inv_freq = ROPE_THETA ** (-jnp.arange(0, HEAD_DIM, 2, dtype=F32) / HEAD_DIM)
    ang = pos[:, None] * inv_freq[None, :]
    cos_h, sin_h = jnp.cos(ang), jnp.sin(ang)
    zero = jnp.zeros_like(sin_h)
    reps = LANES // HEAD_DIM
    cos = jnp.tile(jnp.concatenate([cos_h, cos_h], -1), (1, reps))
    sa = jnp.tile(jnp.concatenate([-sin_h, zero], -1), (1, reps))
    sb = jnp.tile(jnp.concatenate([zero, sin_h], -1), (1, reps))
    nseq = seq // tm
    tab = pl.BlockSpec((tm, LANES), lambda i: (i % nseq, 0))
    n_out = Q_DIM + 2 * KV_DIM
    return pl.pallas_call(
        _qkv_kernel,
        grid=(T // tm,),
        in_specs=[pl.BlockSpec((tm, D_MODEL), lambda i: (i, 0)),
                  pl.BlockSpec((D_MODEL, n_out), lambda i: (0, 0)),
                  tab, tab, tab],
        out_specs=[pl.BlockSpec((tm, Q_DIM), lambda i: (i, 0)),
                   pl.BlockSpec((tm, KV_DIM), lambda i: (i, 0)),
                   pl.BlockSpec((tm, VAUG_DIM), lambda i: (i, 0))],
        out_shape=[jax.ShapeDtypeStruct((T, Q_DIM), BF16),
                   jax.ShapeDtypeStruct((T, KV_DIM), BF16),
                   jax.ShapeDtypeStruct((T, VAUG_DIM), BF16)],
        compiler_params=_params("parallel"),
    )(x2, w_qkv, cos, sa, sb)


ATT_TQ = 1024
ATT_SUB = ATT_TQ // BLOCK
ATT_PAIR = 2
ATT_KEYS = 3 * BLOCK
ATT_ROWS = GROUP * BLOCK


def _attn_kernel(sink_ref, q_ref, kp_ref, km_ref, kn_ref, vp_ref, vm_ref, vn_ref,
                 o_ref, kbuf, vbuf):
    i = pl.program_id(1)
    last = pl.num_programs(1) - 1
    kbuf[0:BLOCK] = kp_ref[...]
    kbuf[BLOCK:BLOCK + ATT_TQ] = km_ref[...]
    kbuf[BLOCK + ATT_TQ:] = kn_ref[...]
    vbuf[0:BLOCK] = vp_ref[...]
    vbuf[BLOCK:BLOCK + ATT_TQ] = vm_ref[...]
    vbuf[BLOCK + ATT_TQ:] = vn_ref[...]

    row = lax.broadcasted_iota(jnp.int32, (ATT_ROWS, BLOCK), 0)
    key = lax.broadcasted_iota(jnp.int32, (ATT_ROWS, BLOCK), 1)
    ql = row & (BLOCK - 1)
    band_prev = key >= ql
    band_next = key <= ql
    hrow = lax.broadcasted_iota(jnp.int32, (ATT_ROWS, 1), 0) // BLOCK
    sinks = []
    for g in range(N_KV_HEADS):
        sk = jnp.zeros((ATT_ROWS, 1), F32)
        for h in range(GROUP):
            sk = jnp.where(hrow == h, sink_ref[g * GROUP + h] * LOG2E, sk)
        sinks.append(sk)

    def blocks(jj, carry):
        pairs = []
        for u in range(ATT_PAIR):
            j = jj * ATT_PAIR + u
            r0 = pl.multiple_of(j * BLOCK, BLOCK)
            has_prev = jnp.logical_not((i == 0) & (j == 0))
            has_next = jnp.logical_not((i == last) & (j == ATT_SUB - 1))
            pairs += [(r0, band_prev & has_prev, band_next & has_next, g)
                      for g in range(N_KV_HEADS)]
        scores = []
        for r0, _, _, g in pairs:
            qb = q_ref[pl.ds(r0, BLOCK), g * GROUP * HEAD_DIM:(g + 1) * GROUP * HEAD_DIM]
            q4 = jnp.concatenate(
                [qb[:, h * HEAD_DIM:(h + 1) * HEAD_DIM] for h in range(GROUP)], axis=0)
            kb = kbuf[pl.ds(r0, ATT_KEYS), g * HEAD_DIM:(g + 1) * HEAD_DIM]
            scores.append(lax.dot_general(q4, kb, (((1,), (1,)), ((), ())),
                                          preferred_element_type=F32))
        probs = []
        for s, (_, valid_prev, valid_next, g) in zip(scores, pairs):
            s = jnp.concatenate(
                [jnp.where(valid_prev, s[:, :BLOCK], NEG_INF), s[:, BLOCK:2 * BLOCK],
                 jnp.where(valid_next, s[:, 2 * BLOCK:], NEG_INF)], axis=1)
            m = jnp.maximum(jnp.max(s, axis=-1, keepdims=True), sinks[g])
            probs.append((jnp.exp2(s - m).astype(BF16), jnp.exp2(sinks[g] - m)))
        for (p, p_sink), (r0, _, _, g) in zip(probs, pairs):
            vb = vbuf[pl.ds(r0, ATT_KEYS), g * LANES:(g + 1) * LANES]
            oa = jnp.dot(p, vb, preferred_element_type=F32)
            o = oa[:, :HEAD_DIM] / (oa[:, HEAD_DIM:] + p_sink)
            o_ref[pl.ds(r0, BLOCK), g * GROUP * HEAD_DIM:(g + 1) * GROUP * HEAD_DIM] = (
                jnp.concatenate([o[h * BLOCK:(h + 1) * BLOCK] for h in range(GROUP)],
                                axis=1).astype(BF16))
        return carry

    lax.fori_loop(0, ATT_SUB // ATT_PAIR, blocks, 0)


def _attention(q, k, v, sink, batch, seq):
    nb = seq // BLOCK
    q3 = q.reshape(batch, seq, Q_DIM)
    k3 = k.reshape(batch, seq, KV_DIM)
    v3 = v.reshape(batch, seq, VAUG_DIM)
    prev = lambda w: pl.BlockSpec((None, BLOCK, w),
                                  lambda b, i, s: (b, jnp.maximum(i * ATT_SUB - 1, 0), 0))
    main = lambda w: pl.BlockSpec((None, ATT_TQ, w), lambda b, i, s: (b, i, 0))
    nxt = lambda w: pl.BlockSpec(
        (None, BLOCK, w), lambda b, i, s: (b, jnp.minimum((i + 1) * ATT_SUB, nb - 1), 0))
    out = pl.pallas_call(
        _attn_kernel,
        grid_spec=pltpu.PrefetchScalarGridSpec(
            num_scalar_prefetch=1,
            grid=(batch, seq // ATT_TQ),
            in_specs=[pl.BlockSpec((None, ATT_TQ, Q_DIM), lambda b, i, s: (b, i, 0)),
                      prev(KV_DIM), main(KV_DIM), nxt(KV_DIM),
                      prev(VAUG_DIM), main(VAUG_DIM), nxt(VAUG_DIM)],
            out_specs=pl.BlockSpec((None, ATT_TQ, Q_DIM), lambda b, i, s: (b, i, 0)),
            scratch_shapes=[pltpu.VMEM((ATT_TQ + 2 * BLOCK, KV_DIM), BF16),
                            pltpu.VMEM((ATT_TQ + 2 * BLOCK, VAUG_DIM), BF16)]),
        out_shape=jax.ShapeDtypeStruct((batch, seq, Q_DIM), BF16),
        compiler_params=_params("parallel", "parallel"),
    )(sink.astype(F32), q3, k3, k3, k3, v3, v3, v3)
    return out.reshape(batch * seq, Q_DIM)


FFN_FC = 512
FFN_TM = 1024
FFN_RC = 256


def _swiglu_hidden(xb, wg_ref, wu_ref):
    gate = jnp.dot(xb, wg_ref[...].astype(BF16), preferred_element_type=F32)
    up = jnp.dot(xb, wu_ref[...].astype(BF16), preferred_element_type=F32)
    return (_silu(gate) * up).astype(BF16)


def _swiglu_step(xb, wg_ref, wu_ref, wd_ref):
    half = wg_ref.shape[1] // 2
    out = None
    for c in range(2):
        cols = slice(c * half, (c + 1) * half)
        gate = jnp.dot(xb, wg_ref[:, cols].astype(BF16), preferred_element_type=F32)
        up = jnp.dot(xb, wu_ref[:, cols].astype(BF16), preferred_element_type=F32)
        h = (_silu(gate) * up).astype(BF16)
        part = jnp.dot(h, wd_ref[cols, :].astype(BF16), preferred_element_type=F32)
        out = part if out is None else out + part
    return out


def _attn_out_ffn_kernel(att_ref, x_ref, wo_ref, g1_ref, b1_ref, wg_ref, wu_ref, wd_ref,
                         g2_ref, b2_ref, o_ref, x1_scr, xb_scr, acc_scr):
    k = pl.program_id(1)

    @pl.when(k == 0)
    def _():
        for r in range(0, att_ref.shape[0], FFN_RC):
            rows = slice(r, r + FFN_RC)
            y = jnp.dot(att_ref[rows, :], wo_ref[...], preferred_element_type=F32)
            x1 = _layer_norm(DN_ALPHA * x_ref[rows, :] + y, g1_ref[...], b1_ref[...])
            x1_scr[rows, :] = x1
            xb_scr[rows, :] = x1.astype(BF16)
        acc_scr[...] = jnp.zeros_like(acc_scr)

    last = pl.num_programs(1) - 1

    @pl.when(k < last)
    def _():
        acc_scr[...] += _swiglu_step(xb_scr[...], wg_ref, wu_ref, wd_ref)

    @pl.when(k == last)
    def _():
        h = _swiglu_hidden(xb_scr[...], wg_ref, wu_ref)
        wd = wd_ref[...].astype(BF16)
        for r in range(0, h.shape[0], FFN_RC):
            rows = slice(r, r + FFN_RC)
            ffn = acc_scr[rows, :] + jnp.dot(h[rows, :], wd, preferred_element_type=F32)
            y = DN_ALPHA * x1_scr[rows, :] + ffn
            o_ref[rows, :] = _layer_norm(y, g2_ref[...], b2_ref[...])


def _attn_out_ffn(att, x2, wo, g1, b1, wg, wu, wd, g2, b2):
    T = x2.shape[0]
    tm, fc = min(FFN_TM, T), FFN_FC
    vec = pl.BlockSpec((1, D_MODEL), lambda i, k: (0, 0))
    row = pl.BlockSpec((tm, D_MODEL), lambda i, k: (i, 0))
    return pl.pallas_call(
        _attn_out_ffn_kernel,
        grid=(T // tm, D_FF // fc),
        in_specs=[row, row,
                  pl.BlockSpec((D_MODEL, D_MODEL), lambda i, k: (0, 0)),
                  vec, vec,
                  pl.BlockSpec((D_MODEL, fc), lambda i, k: (0, k)),
                  pl.BlockSpec((D_MODEL, fc), lambda i, k: (0, k)),
                  pl.BlockSpec((fc, D_MODEL), lambda i, k: (k, 0)),
                  vec, vec],
        out_specs=row,
        out_shape=jax.ShapeDtypeStruct((T, D_MODEL), F32),
        scratch_shapes=[pltpu.VMEM((tm, D_MODEL), F32), pltpu.VMEM((tm, D_MODEL), BF16),
                        pltpu.VMEM((tm, D_MODEL), F32)],
        compiler_params=_params("parallel", "arbitrary"),
    )(att, x2, wo, g1.reshape(1, -1), b1.reshape(1, -1), wg, wu, wd,
      g2.reshape(1, -1), b2.reshape(1, -1))


def _pw1_glu_kernel(x_ref, w_ref, bias_ref, o_ref):
    h = jnp.dot(x_ref[...].astype(BF16), w_ref[...], preferred_element_type=F32) + bias_ref[...]
    o_ref[...] = h[:, :D_MODEL] * jax.nn.sigmoid(h[:, D_MODEL:])


def _pw1_glu(x2, w, bias, tm=1024):
    T = x2.shape[0]
    return pl.pallas_call(
        _pw1_glu_kernel,
        grid=(T // tm,),
        in_specs=[pl.BlockSpec((tm, D_MODEL), lambda i: (i, 0)),
                  pl.BlockSpec((D_MODEL, 2 * D_MODEL), lambda i: (0, 0)),
                  pl.BlockSpec((1, 2 * D_MODEL), lambda i: (0, 0))],
        out_specs=pl.BlockSpec((tm, D_MODEL), lambda i: (i, 0)),
        out_shape=jax.ShapeDtypeStruct((T, D_MODEL), F32),
        compiler_params=_params("parallel"),
    )(x2, w, bias.reshape(1, -1))


CONV_TM = 512
CONV_HALO = 16
CONV_RC = 64
CONV_PARTS = 2
CONV_SHIFT_ROWS = CONV_TM + 2 * CONV_HALO - SUBLANES


def _conv_kernel(hp_ref, hm_ref, hn_ref, x_ref, dw_ref, dwb_ref, ng_ref, nb_ref,
                 w2_ref, b2_ref, g_ref, b_ref, o_ref, hbuf, hshift, cbuf):
    i = pl.program_id(1)
    last = pl.num_programs(1) - 1
    tm = CONV_TM
    hbuf[0:CONV_HALO] = jnp.where(i > 0, hp_ref[...], 0.0)
    hbuf[CONV_HALO:CONV_HALO + tm] = hm_ref[...]
    hbuf[CONV_HALO + tm:] = jnp.where(i < last, hn_ref[...], 0.0)
    for sh in range(1, SUBLANES):
        hshift[sh - 1] = hbuf[sh:sh + CONV_SHIFT_ROWS, :]
    off = CONV_HALO - CONV_HALF

    def rows(r, carry):
        r0 = pl.multiple_of(r * CONV_RC, CONV_RC)
        for c in range(SEGS):
            cols = slice(c * LANES, (c + 1) * LANES)
            parts = [None] * CONV_PARTS
            for sh in range(SUBLANES):
                taps = [w for w in range(CONV_WIDTH) if (off + w) % SUBLANES == sh]
                src = hbuf if sh == 0 else hshift.at[sh - 1]
                span = (off + taps[-1]) // SUBLANES * SUBLANES + CONV_RC
                seg = src[pl.ds(r0, span), cols]
                for w in taps:
                    lo = (off + w) // SUBLANES * SUBLANES
                    term = seg[lo:lo + CONV_RC] * dw_ref[w:w + 1, cols]
                    k = w % CONV_PARTS
                    parts[k] = term if parts[k] is None else parts[k] + term
            acc = parts[0]
            for part in parts[1:]:
                acc = acc + part
            cbuf[pl.ds(r0, CONV_RC), cols] = acc + dwb_ref[:, cols]
        return carry

    lax.fori_loop(0, tm // CONV_RC, rows, 0)
    h = _silu(_layer_norm(cbuf[...], ng_ref[...], nb_ref[...]))
    y = jnp.dot(h.astype(BF16), w2_ref[...], preferred_element_type=F32) + b2_ref[...]
    y = DN_ALPHA * x_ref[...] + y
    o_ref[...] = _layer_norm(y, g_ref[...], b_ref[...])


def _conv_block(h, x2, dw_w, dw_b, ng, nb, w2, b2, g, b, batch, seq):
    tm = CONV_TM
    h3 = h.reshape(batch, seq, D_MODEL)
    x3 = x2.reshape(batch, seq, D_MODEL)
    per = tm // CONV_HALO
    nh = seq // CONV_HALO
    vec = pl.BlockSpec((1, D_MODEL), lambda bb, i: (0, 0))
    out = pl.pallas_call(
        _conv_kernel,
        grid=(batch, seq // tm),
        in_specs=[pl.BlockSpec((None, CONV_HALO, D_MODEL),
                               lambda bb, i: (bb, jnp.maximum(i * per - 1, 0), 0)),
                  pl.BlockSpec((None, tm, D_MODEL), lambda bb, i: (bb, i, 0)),
                  pl.BlockSpec((None, CONV_HALO, D_MODEL),
                               lambda bb, i: (bb, jnp.minimum((i + 1) * per, nh - 1), 0)),
                  pl.BlockSpec((None, tm, D_MODEL), lambda bb, i: (bb, i, 0)),
                  pl.BlockSpec((CONV_WIDTH, D_MODEL), lambda bb, i: (0, 0)),
                  vec, vec, vec,
                  pl.BlockSpec((D_MODEL, D_MODEL), lambda bb, i: (0, 0)),
                  vec, vec, vec],
        out_specs=pl.BlockSpec((None, tm, D_MODEL), lambda bb, i: (bb, i, 0)),
        out_shape=jax.ShapeDtypeStruct((batch, seq, D_MODEL), F32),
        scratch_shapes=[pltpu.VMEM((tm + 2 * CONV_HALO, D_MODEL), F32),
                        pltpu.VMEM((SUBLANES - 1, CONV_SHIFT_ROWS, D_MODEL), F32),
                        pltpu.VMEM((tm, D_MODEL), F32)],
        compiler_params=_params("parallel", "parallel"),
    )(h3, h3, h3, x3, dw_w, dw_b.reshape(1, -1), ng.reshape(1, -1), nb.reshape(1, -1),
      w2, b2.reshape(1, -1), g.reshape(1, -1), b.reshape(1, -1))
    return out.reshape(batch * seq, D_MODEL)


ROUTE_TS = 512
ROUTE_BITS = ROUTE_TS.bit_length()
META_E1, META_E2, META_R1, META_R2, META_G1, META_G2 = range(6)


def _router_kernel(x_ref, w_ref, meta_t_ref, base_ref, cnt_ref, base_scr):
    t = pl.program_id(0)

    @pl.when(t == 0)
    def _():
        base_scr[...] = jnp.zeros_like(base_scr)

    base_ref[...] = base_scr[...]

    tm = x_ref.shape[0]
    x = x_ref[...]
    xh = x.astype(BF16)
    xl = (x - xh.astype(F32)).astype(BF16)
    parts = jnp.dot(jnp.concatenate([xh, xl], axis=0), w_ref[...],
                    preferred_element_type=F32)
    logits = ((parts[:tm, :LANES] + parts[:tm, LANES:])
              + (parts[tm:, :LANES] + parts[tm:, LANES:]))
    lane = lax.broadcasted_iota(jnp.int32, (tm, LANES), 1)
    lane_f = lane.astype(F32)
    lg = jnp.where(lane < N_EXPERTS, logits, -jnp.inf)
    m1 = jnp.max(lg, axis=-1, keepdims=True)
    i1 = jnp.min(jnp.where(lg == m1, lane_f, float(LANES)), axis=-1, keepdims=True)
    oh1 = lane_f == i1
    lg2 = jnp.where(oh1, -jnp.inf, lg)
    m2 = jnp.max(lg2, axis=-1, keepdims=True)
    i2 = jnp.min(jnp.where(lg2 == m2, lane_f, float(LANES)), axis=-1, keepdims=True)
    oh2 = lane_f == i2
    e = jnp.exp(m2 - m1)
    g1 = 1.0 / (1.0 + e)
    g2 = e / (1.0 + e)
    oh = jnp.where(oh1 | oh2, 1.0, 0.0)
    r = lax.broadcasted_iota(jnp.int32, (tm, tm), 0)
    c = lax.broadcasted_iota(jnp.int32, (tm, tm), 1)
    lower = jnp.where(c < r, 1.0, 0.0).astype(BF16)
    before = jnp.dot(lower, oh.astype(BF16), preferred_element_type=F32) + base_scr[...]
    rank1 = jnp.sum(jnp.where(oh1, before, 0.0), axis=-1, keepdims=True)
    rank2 = jnp.sum(jnp.where(oh2, before, 0.0), axis=-1, keepdims=True)
    base_scr[...] += jnp.sum(oh, axis=0, keepdims=True)
    meta = jnp.zeros((tm, LANES), F32)
    for idx, val in ((META_E1, i1), (META_E2, i2),
                     (META_R1, rank1), (META_R2, rank2), (META_G1, g1), (META_G2, g2)):
        meta = jnp.where(lane == idx, val, meta)
    meta_t_ref[...] = jnp.transpose(meta)[:SUBLANES]
    cnt_ref[...] = base_scr[...]


def _router(x2, w_router):
    T = x2.shape[0]
    tm = min(ROUTE_TS, T)
    w = jnp.zeros((D_MODEL, LANES), F32).at[:, :N_EXPERTS].set(w_router.astype(F32))
    w_hi = w.astype(BF16)
    w_lo = (w - w_hi.astype(F32)).astype(BF16)
    w = jnp.concatenate([w_hi, w_lo], axis=1)
    return pl.pallas_call(
        _router_kernel,
        grid=(T // tm,),
        in_specs=[pl.BlockSpec((tm, D_MODEL), lambda i: (i, 0)),
                  pl.BlockSpec((D_MODEL, 2 * LANES), lambda i: (0, 0))],
        out_specs=[pl.BlockSpec((SUBLANES, tm), lambda i: (0, i)),
                   pl.BlockSpec((None, 1, LANES), lambda i: (i, 0, 0)),
                   pl.BlockSpec((1, LANES), lambda i: (0, 0))],
        out_shape=[jax.ShapeDtypeStruct((SUBLANES, T), F32),
                   jax.ShapeDtypeStruct((T // tm, 1, LANES), F32),
                   jax.ShapeDtypeStruct((1, LANES), F32)],
        scratch_shapes=[pltpu.VMEM((1, LANES), F32)],
        compiler_params=_params("arbitrary"),
    )(x2, w)


DISP_ZB = 128


class _Runs(NamedTuple):
    slot8: jax.Array
    n: jax.Array
    off8: jax.Array


def _row_copy(src, src_row8, dst, dst_row8, sem):
    return pltpu.make_async_copy(
        src.at[pl.ds(pl.multiple_of(src_row8, SUBLANES), SEGS), :],
        dst.at[pl.ds(pl.multiple_of(dst_row8, SUBLANES), SEGS), :], sem)


def _run_copies(n, src, src0_8, dst, dst0_8, sem, priority):
    for bit in range(ROUTE_BITS):
        size = 1 << bit

        @pl.when((n & size) != 0)
        def _():
            above8 = ((n >> (bit + 1)) << (bit + 1)) * SEGS
            pltpu.make_async_copy(
                src.at[pl.ds(pl.multiple_of(src0_8 + above8, SUBLANES), size * SEGS), :],
                dst.at[pl.ds(pl.multiple_of(dst0_8 + above8, SUBLANES), size * SEGS), :],
                sem).start(priority=priority)


ROUTE_UNROLL = 8


def _dispatch_kernel(pad0_ref, padn_ref, tail_ref, run_dst_ref, run_n_ref, run_off_ref,
                     s1_ref, s2_ref, x_ref, xs_ref, sbuf, stage, zbuf, sem, zsem):
    i = pl.program_id(0)
    ts = x_ref.shape[0]
    slot = i % 2
    runs = stage.at[slot]
    for s in range(SEGS):
        sbuf[pl.ds(s, ts, stride=SEGS), :] = x_ref[:, s * LANES:(s + 1) * LANES]

    def place(c, carry):
        for u in range(ROUTE_UNROLL):
            r = c * ROUTE_UNROLL + u
            tile = sbuf[pl.ds(pl.multiple_of(r * SEGS, SUBLANES), SEGS), :]
            runs[pl.ds(pl.multiple_of(s1_ref[r], SUBLANES), SEGS), :] = tile
            runs[pl.ds(pl.multiple_of(s2_ref[r], SUBLANES), SEGS), :] = tile
        return carry

    lax.fori_loop(0, ts // ROUTE_UNROLL, place, 0)
    for e in range(N_EXPERTS):
        idx = i * N_EXPERTS + e
        _run_copies(run_n_ref[idx], runs, run_off_ref[idx], xs_ref, run_dst_ref[idx],
                    sem.at[slot], e % 2)

    def wait_step(which):
        pltpu.make_async_copy(stage.at[which], xs_ref.at[pl.ds(0, 2 * ts * SEGS), :],
                              sem.at[which]).wait()

    @pl.when(pl.program_id(0) == 0)
    def _():
        zbuf[...] = jnp.zeros_like(zbuf)
        for e in range(N_EXPERTS):
            def zstart(j, carry, e=e):
                _row_copy(zbuf, 0, xs_ref, pad0_ref[e] + j * SEGS, zsem).start()
                return carry

            def zwait(j, carry, e=e):
                _row_copy(zbuf, 0, xs_ref, pad0_ref[e] + j * SEGS, zsem).wait()
                return carry

            lax.fori_loop(0, padn_ref[e], zstart, 0)
            lax.fori_loop(0, padn_ref[e], zwait, 0)

        def tail_copy(j):
            rows = DISP_ZB * SEGS
            dst0 = pl.multiple_of(tail_ref[0] + j * rows, SUBLANES)
            return pltpu.make_async_copy(zbuf, xs_ref.at[pl.ds(dst0, rows), :], zsem)

        def tstart(j, carry):
            tail_copy(j).start()
            return carry

        def twait(j, carry):
            tail_copy(j).wait()
            return carry

        lax.fori_loop(0, tail_ref[1], tstart, 0)
        lax.fori_loop(0, tail_ref[1], twait, 0)

    @pl.when(i > 0)
    def _():
        wait_step(1 - slot)

    @pl.when(i == pl.num_programs(0) - 1)
    def _():
        wait_step(slot)


def _dispatch(x2, runs, sidx1_8, sidx2_8, pad0_8, padn, tail, n_slots):
    T = x2.shape[0]
    ts = min(ROUTE_TS, T)
    smem = lambda: pl.BlockSpec((ts,), lambda i, *_: (i,), memory_space=pltpu.SMEM)
    return pl.pallas_call(
        _dispatch_kernel,
        grid_spec=pltpu.PrefetchScalarGridSpec(
            num_scalar_prefetch=6,
            grid=(T // ts,),
            in_specs=[smem(), smem(),
                      pl.BlockSpec((ts, D_MODEL), lambda i, *_: (i, 0))],
            out_specs=pl.BlockSpec(memory_space=pl.ANY),
            scratch_shapes=[pltpu.VMEM((ts * SEGS, LANES), F32),
                            pltpu.VMEM((2, 2 * ts * SEGS, LANES), F32),
                            pltpu.VMEM((DISP_ZB * SEGS, LANES), F32),
                            pltpu.SemaphoreType.DMA((2,)),
                            pltpu.SemaphoreType.DMA(())]),
        out_shape=jax.ShapeDtypeStruct((n_slots * SEGS, LANES), F32),
        compiler_params=_params("arbitrary"),
    )(pad0_8, padn, tail, runs.slot8, runs.n, runs.off8, sidx1_8, sidx2_8, x2)


MOE_TM = 1024
MOE_FC = 512
MOE_RC = 256


def _moe_kernel(te_ref, nu_ref, xs_ref, wg_ref, wu_ref, wd_ref, o_ref, xb_scr, acc_scr):
    i = pl.program_id(0)
    k = pl.program_id(1)
    tm = xb_scr.shape[0]

    @pl.when(i < nu_ref[0])
    def _():
        last = pl.num_programs(1) - 1
        rc = min(MOE_RC, tm)

        @pl.when(k == 0)
        def _():
            wg = wg_ref[...].astype(BF16)
            wu = wu_ref[...].astype(BF16)
            wd = wd_ref[...].astype(BF16)
            for r in range(0, tm, rc):
                rows = slice(r, r + rc)
                xb = jnp.concatenate(
                    [xs_ref[pl.ds(r * SEGS + s, rc, stride=SEGS), :] for s in range(SEGS)],
                    axis=1).astype(BF16)
                xb_scr[rows, :] = xb
                gate = jnp.dot(xb, wg, preferred_element_type=F32)
                up = jnp.dot(xb, wu, preferred_element_type=F32)
                h = (_silu(gate) * up).astype(BF16)
                acc_scr[rows, :] = jnp.dot(h, wd, preferred_element_type=F32)

        @pl.when((k > 0) & (k < last))
        def _():
            acc_scr[...] += _swiglu_step(xb_scr[...], wg_ref, wu_ref, wd_ref)

        @pl.when(k == last)
        def _():
            h = _swiglu_hidden(xb_scr[...], wg_ref, wu_ref)
            wd = wd_ref[...].astype(BF16)
            for r in range(0, tm, rc):
                rows = slice(r, r + rc)
                y = acc_scr[rows, :] + jnp.dot(h[rows, :], wd, preferred_element_type=F32)
                for s in range(SEGS):
                    o_ref[pl.ds(r * SEGS + s, rc, stride=SEGS), :] = (
                        y[:, s * LANES:(s + 1) * LANES])

    @pl.when((i >= nu_ref[0]) & (k == 0))
    def _():
        o_ref[...] = jnp.zeros_like(o_ref)


def _moe(xs, tile_expert, n_used, wg, wu, wd, tm):
    n_tiles = xs.shape[0] // (tm * SEGS)
    fc = MOE_FC
    nk = D_FF // fc

    def tile(i, k, te, nu):
        return (jnp.minimum(i, nu[0] - 1), 0)

    def kk(i, k, nu):
        return jnp.where(i < nu[0], k, nk - 1)

    return pl.pallas_call(
        _moe_kernel,
        grid_spec=pltpu.PrefetchScalarGridSpec(
            num_scalar_prefetch=2,
            grid=(n_tiles, nk),
            in_specs=[pl.BlockSpec((tm * SEGS, LANES), tile),
                      pl.BlockSpec((None, D_MODEL, fc),
                                   lambda i, k, te, nu: (te[i], 0, kk(i, k, nu))),
                      pl.BlockSpec((None, D_MODEL, fc),
                                   lambda i, k, te, nu: (te[i], 0, kk(i, k, nu))),
                      pl.BlockSpec((None, fc, D_MODEL),
                                   lambda i, k, te, nu: (te[i], kk(i, k, nu), 0))],
            out_specs=pl.BlockSpec((tm * SEGS, LANES), lambda i, k, te, nu: (i, 0)),
            scratch_shapes=[pltpu.VMEM((tm, D_MODEL), BF16), pltpu.VMEM((tm, D_MODEL), F32)]),
        out_shape=jax.ShapeDtypeStruct(xs.shape, F32),
        compiler_params=_params("arbitrary", "arbitrary"),
    )(tile_expert, n_used, xs, wg, wu, wd)


def _combine_kernel(run_src_ref, run_n_ref, run_off_ref, s1_ref, s2_ref, g1_ref, g2_ref,
                    ys_ref, x_ref, g_ref, b_ref, o_ref, stage, cbuf, sem):
    i = pl.program_id(0)
    ts = x_ref.shape[0]
    slot = i % 2

    def fetch(step, which):
        for e in range(N_EXPERTS):
            idx = step * N_EXPERTS + e
            _run_copies(run_n_ref[idx], ys_ref, run_src_ref[idx], stage.at[which],
                        run_off_ref[idx], sem.at[which], e % 2)

    @pl.when(i == 0)
    def _():
        fetch(i, slot)

    @pl.when(i < pl.num_programs(0) - 1)
    def _():
        fetch(i + 1, 1 - slot)

    pltpu.make_async_copy(ys_ref.at[pl.ds(0, 2 * ts * SEGS), :], stage.at[slot],
                          sem.at[slot]).wait()
    runs = stage.at[slot]

    def mix(c, carry):
        for u in range(ROUTE_UNROLL):
            r = c * ROUTE_UNROLL + u
            a = runs[pl.ds(pl.multiple_of(s1_ref[r], SUBLANES), SEGS), :]
            bb = runs[pl.ds(pl.multiple_of(s2_ref[r], SUBLANES), SEGS), :]
            cbuf[pl.ds(pl.multiple_of(r * SEGS, SUBLANES), SEGS), :] = (
                g1_ref[r] * a + g2_ref[r] * bb)
        return carry

    lax.fori_loop(0, ts // ROUTE_UNROLL, mix, 0)
    y = jnp.concatenate(
        [DN_ALPHA * x_ref[:, s * LANES:(s + 1) * LANES] + cbuf[pl.ds(s, ts, stride=SEGS), :]
         for s in range(SEGS)], axis=1)
    o_ref[...] = _layer_norm(y, g_ref[...], b_ref[...])


def _combine(ys, runs, sidx1_8, sidx2_8, gate1, gate2, x2, g, b):
    T = x2.shape[0]
    ts = min(ROUTE_TS, T)
    smem = lambda: pl.BlockSpec((ts,), lambda i, *_: (i,), memory_space=pltpu.SMEM)
    vec = pl.BlockSpec((1, D_MODEL), lambda i, *_: (0, 0))
    return pl.pallas_call(
        _combine_kernel,
        grid_spec=pltpu.PrefetchScalarGridSpec(
            num_scalar_prefetch=3,
            grid=(T // ts,),
            in_specs=[smem(), smem(), smem(), smem(),
                      pl.BlockSpec(memory_space=pl.ANY),
                      pl.BlockSpec((ts, D_MODEL), lambda i, *_: (i, 0)),
                      vec, vec],
            out_specs=pl.BlockSpec((ts, D_MODEL), lambda i, *_: (i, 0)),
            scratch_shapes=[pltpu.VMEM((2, 2 * ts * SEGS, LANES), F32),
                            pltpu.VMEM((ts * SEGS, LANES), F32),
                            pltpu.SemaphoreType.DMA((2,))]),
        out_shape=jax.ShapeDtypeStruct((T, D_MODEL), F32),
        compiler_params=_params("arbitrary"),
    )(runs.slot8, runs.n, runs.off8, sidx1_8, sidx2_8, gate1, gate2, ys, x2,
      g.reshape(1, -1), b.reshape(1, -1))


def _moe_block(x2, w_router, wg, wu, wd, g, b):
    T = x2.shape[0]
    tm = min(MOE_TM, T)
    ts = min(ROUTE_TS, T)
    meta_t, base, cnt = _router(x2, w_router)
    field = lambda idx: meta_t[idx].astype(jnp.int32)
    e1, e2, r1, r2 = field(META_E1), field(META_E2), field(META_R1), field(META_R2)
    counts = cnt[0, :N_EXPERTS].astype(jnp.int32)
    tiles_e = (counts + tm - 1) // tm
    padded = tiles_e * tm
    starts = jnp.cumsum(padded) - padded
    tile_end = jnp.cumsum(tiles_e)
    n_used = tile_end[-1]
    n_tiles = (TOPK_SLOTS * T) // tm + N_EXPERTS
    tid = jnp.minimum(jnp.arange(n_tiles, dtype=jnp.int32), n_used - 1)
    tile_expert = jnp.sum((tid[:, None] >= tile_end[None, :]).astype(jnp.int32), axis=1)
    before = base[:, 0, :N_EXPERTS].astype(jnp.int32)
    run_n = jnp.concatenate([before[1:], counts[None, :]], axis=0) - before
    run_off = jnp.cumsum(run_n, axis=1) - run_n
    runs = _Runs(slot8=((starts[None, :] + before) * SEGS).reshape(-1),
                 n=run_n.reshape(-1), off8=(run_off * SEGS).reshape(-1))
    to_stage = run_off - before

    def stage_row8(e, r):
        e = e.reshape(-1, ts)
        sel = sum(jnp.where(e == k, to_stage[:, k:k + 1], 0) for k in range(N_EXPERTS))
        return ((sel + r.reshape(-1, ts)) * SEGS).reshape(-1)

    sidx1_8 = stage_row8(e1, r1)
    sidx2_8 = stage_row8(e2, r2)
    pad0_8 = (starts + counts) * SEGS
    padn = padded - counts
    tail = jnp.stack([n_used * (tm * SEGS), (n_tiles - n_used) * (tm // DISP_ZB)])
    xs = _dispatch(x2, runs, sidx1_8, sidx2_8, pad0_8, padn, tail, n_tiles * tm)
    ys = _moe(xs, tile_expert, n_used.reshape(1), wg, wu, wd, tm)
    return _combine(ys, runs, sidx1_8, sidx2_8, meta_t[META_G1], meta_t[META_G2], x2, g, b)


TOPK_SLOTS = 2


def kernel(x, attn_w_qkv, attn_sink, attn_w_o, conv_pw1_w, conv_pw1_b, conv_dw_w, conv_dw_b,
           conv_norm_g, conv_norm_b, conv_pw2_w, conv_pw2_b, ffn_w_gate, ffn_w_up, ffn_w_down,
           moe_router, moe_w_gate, moe_w_up, moe_w_down, ln_mix_g, ln_mix_b, ln_ffn_g, ln_ffn_b):
    batch, seq, _ = x.shape
    x2 = x.reshape(batch * seq, D_MODEL)

    q, k, v = _qkv_rope(x2, attn_w_qkv[0].astype(BF16), seq)
    att = _attention(q, k, v, attn_sink[0], batch, seq)
    x2 = _attn_out_ffn(att, x2, attn_w_o[0].astype(BF16), ln_mix_g[0], ln_mix_b[0],
                       ffn_w_gate[0], ffn_w_up[0], ffn_w_down[0], ln_ffn_g[0], ln_ffn_b[0])

    h = _pw1_glu(x2, conv_pw1_w[0].astype(BF16), conv_pw1_b[0])
    x2 = _conv_block(h, x2, conv_dw_w[0], conv_dw_b[0], conv_norm_g[0], conv_norm_b[0],
                     conv_pw2_w[0].astype(BF16), conv_pw2_b[0], ln_mix_g[1], ln_mix_b[1],
                     batch, seq)
    x2 = _moe_block(x2, moe_router[0], moe_w_gate[0], moe_w_up[0], moe_w_down[0],
                    ln_ffn_g[1], ln_ffn_b[1])
    return x2.reshape(batch, seq, D_MODEL)
```
